```python
import jax, jax.numpy as jnp
from jax import lax
import numpy as np

D_MODEL = 4096
BATCH = 4
SEQ = 2048
DEPTH = 2
DEC_BATCH = 8
DEC_SEQ = 8
PAST_LEN = 16384
PAGE_SIZE = 128

N_MIXERS = 2
N_ATTN_LAYERS = (DEPTH + 1) // 2
N_CONV_LAYERS = DEPTH // 2
HEAD_DIM = 128
HEADS_PER_GROUP = 16
DILATED_GROUPS = ((128, 1), (512, 4), (2048, 16))
N_GROUPS = len(DILATED_GROUPS)
ATTN_WIDTH = HEADS_PER_GROUP * HEAD_DIM
QKV_WIDTH = N_GROUPS * 3 * ATTN_WIDTH
Q_BLOCK = 128
CONF_KERNEL = 31
FFN_KERNEL = 3
D_FF = 11008
RMS_EPS = 1e-6
LN_EPS = 1e-5

kernel_name = 'hybrid_dilated_attn_conformer_convffn_step'


def rms_norm(x):
    xf = x.astype(jnp.float32)
    return (xf * lax.rsqrt(jnp.mean(xf * xf, axis=-1, keepdims=True) + RMS_EPS)).astype(x.dtype)


def layer_norm(x, g, b):
    xf = x.astype(jnp.float32)
    mu = jnp.mean(xf, axis=-1, keepdims=True)
    var = jnp.mean(jnp.square(xf - mu), axis=-1, keepdims=True)
    y = (xf - mu) * lax.rsqrt(var + LN_EPS) * g.astype(jnp.float32) + b.astype(jnp.float32)
    return y.astype(x.dtype)


def modulate(x, shift, scale):
    return rms_norm(x) * (1 + scale[:, None, :]) + shift[:, None, :]


def adaln_params(c, w, b):
    return jnp.split(jax.nn.silu(c) @ w + b, 6, axis=-1)


def depthwise_conv(x, w):
    return lax.conv_general_dilated(x, w[:, None, :].astype(x.dtype), window_strides=(1,), padding='VALID',
                                    dimension_numbers=('NWC', 'WIO', 'NWC'),
                                    feature_group_count=x.shape[-1])


def softmax_attend(s, v, spec):
    m = jnp.max(s, axis=-1, keepdims=True)
    p = jnp.exp(s - m)
    den = jnp.sum(p, axis=-1, keepdims=True)
    o = jnp.einsum(spec, p / den, v.astype(jnp.float32))
    return o, (m + jnp.log(den))[..., 0]


def dilated_attn_prompt(q, k, v, dilation, n_steps):
    B, S, H, Dh = q.shape
    L = S // dilation
    nb = -(-L // Q_BLOCK)
    Lp = nb * Q_BLOCK

    def to_sub(a):
        a = a.reshape(B, L, dilation, H, Dh).swapaxes(1, 2).reshape(B * dilation, L, H, Dh)
        return jnp.pad(a, ((0, 0), (0, Lp - L), (0, 0), (0, 0)))

    def from_sub(a):
        rest = a.shape[2:]
        return a[:, :L].reshape((B, dilation, L) + rest).swapaxes(1, 2).reshape((B, S) + rest)

    def band(a):
        ab = a.reshape(-1, nb, Q_BLOCK, H, Dh)
        prev = jnp.pad(ab, ((0, 0), (1, 0), (0, 0), (0, 0), (0, 0)))[:, :-1]
        return jnp.concatenate([prev, ab], axis=2)

    qb = to_sub(q).reshape(-1, nb, Q_BLOCK, H, Dh)
    kb, vb = band(to_sub(k)), band(to_sub(v))
    s = jnp.einsum('bnqhd,bnkhd->bnhqk', qb, kb, preferred_element_type=jnp.float32) * (Dh ** -0.5)
    qi = jnp.arange(Q_BLOCK)[:, None]
    ki = jnp.arange(2 * Q_BLOCK)[None, :]
    dist = qi + Q_BLOCK - ki
    key_idx = jnp.arange(nb)[:, None, None] * Q_BLOCK - Q_BLOCK + ki[None]
    valid = (dist >= 0) & (dist <= n_steps) & (key_idx >= 0)
    s = jnp.where(valid[None, :, None], s, -jnp.inf)
    o, lse = softmax_attend(s, vb, 'bnhqk,bnkhd->bnqhd')
    o = o.reshape(-1, Lp, H, Dh)
    lse = lse.swapaxes(2, 3).reshape(-1, Lp, H)
    return from_sub(o), from_sub(lse)


def dilated_attn_sample(q, k_all, v_all, buf_len, dilation, n_steps):
    T = q.shape[1]
    idx = buf_len + jnp.arange(T)[:, None] - dilation * jnp.arange(n_steps + 1)[None, :]
    valid = idx >= 0
    idx = jnp.maximum(idx, 0)
    kg = k_all[:, idx]
    vg = v_all[:, idx]
    s = jnp.einsum('bthd,btjhd->bhtj', q, kg, preferred_element_type=jnp.float32) * (q.shape[-1] ** -0.5)
    s = jnp.where(valid[None, None], s, -jnp.inf)
    o, lse = softmax_attend(s, vg, 'bhtj,btjhd->bthd')
    return o, lse.swapaxes(1, 2)


def dilated_mixer(h, w_qkv, w_o, caches):
    B, S, _ = h.shape
    qkv = (h @ w_qkv).reshape(B, S, N_GROUPS, 3, HEADS_PER_GROUP, HEAD_DIM)
    outs, lses, new_states = [], [], []
    for g, (window, dilation) in enumerate(DILATED_GROUPS):
        q, k, v = qkv[:, :, g, 0], qkv[:, :, g, 1], qkv[:, :, g, 2]
        n_steps = window // dilation
        if caches is None:
            o, lse = dilated_attn_prompt(q, k, v, dilation, n_steps)
            keep = min(window, S)
            new_states.append(jnp.stack([k[:, S - keep:], v[:, S - keep:]], axis=2))
        else:
            buf = caches[g]
            wb = buf.shape[1]
            k_all = jnp.concatenate([buf[:, :, 0].astype(k.dtype), k], axis=1)
            v_all = jnp.concatenate([buf[:, :, 1].astype(v.dtype), v], axis=1)
            o, lse = dilated_attn_sample(q, k_all, v_all, wb, dilation, n_steps)
            new_states.append(jnp.stack([k_all[:, -wb:], v_all[:, -wb:]], axis=2))
        outs.append(o)
        lses.append(lse)
    wts = jax.nn.softmax(jnp.stack(lses), axis=0)
    o = jnp.einsum('gbsh,gbshd->bshd', wts, jnp.stack(outs))
    return o.reshape(B, S, ATTN_WIDTH).astype(h.dtype) @ w_o, new_states


def conformer_conv_module(h, w_pw1, b_pw1, w_dw, b_dw, ln_g, ln_b, w_pw2, b_pw2, buf):
    B, S, D = h.shape
    a, gt = jnp.split(h @ w_pw1 + b_pw1, 2, axis=-1)
    u = a * jax.nn.sigmoid(gt)
    ctx = jnp.zeros((B, CONF_KERNEL - 1, D), u.dtype) if buf is None else buf.astype(u.dtype)
    u_all = jnp.concatenate([ctx, u], axis=1)
    y = depthwise_conv(u_all, w_dw) + b_dw
    y = jax.nn.silu(layer_norm(y, ln_g, ln_b))
    return y @ w_pw2 + b_pw2, u_all[:, -(CONF_KERNEL - 1):]


def conv_ffn(h, w_up, w_dw, w_down, buf):
    B = h.shape[0]
    u = h @ w_up
    ctx = jnp.zeros((B, FFN_KERNEL - 1, u.shape[-1]), u.dtype) if buf is None else buf.astype(u.dtype)
    u_all = jnp.concatenate([ctx, u], axis=1)
    gate, val = jnp.split(depthwise_conv(u_all, w_dw), 2, axis=-1)
    return (jax.nn.silu(gate) * val) @ w_down, u_all[:, -(FFN_KERNEL - 1):]


def trunk(x, c, kv_caches, conv_states, ffn_states, params):
    (w_ada, b_ada, w_qkv, w_o, w_pw1, b_pw1, w_dw, b_dw, ln_g, ln_b, w_pw2, b_pw2,
     w_up, w_ffn_dw, w_down, final_norm_g) = params
    new_kv = [[] for _ in range(N_GROUPS)]
    new_conv, new_ffn = [], []
    for i in range(DEPTH):
        sh1, sc1, gt1, sh2, sc2, gt2 = adaln_params(c, w_ada[i], b_ada[i])
        h = modulate(x, sh1, sc1)
        j = i // N_MIXERS
        if i % N_MIXERS == 0:
            caches = None if kv_caches is None else [cache[j] for cache in kv_caches]
            mix, states = dilated_mixer(h, w_qkv[j], w_o[j], caches)
            for lst, st in zip(new_kv, states):
                lst.append(st)
        else:
            buf = None if conv_states is None else conv_states[j]
            mix, st = conformer_conv_module(h, w_pw1[j], b_pw1[j], w_dw[j], b_dw[j], ln_g[j], ln_b[j],
                                            w_pw2[j], b_pw2[j], buf)
            new_conv.append(st)
        x = x + gt1[:, None, :] * mix
        h = modulate(x, sh2, sc2)
        buf = None if ffn_states is None else ffn_states[i]
        ff, st = conv_ffn(h, w_up[i], w_ffn_dw[i], w_down[i], buf)
        new_ffn.append(st)
        x = x + gt2[:, None, :] * ff
    y = rms_norm(x) * final_norm_g
    kv_out = [jnp.stack(lst) for lst in new_kv]
    return y, kv_out, jnp.stack(new_conv), jnp.stack(new_ffn)


def _normal(k, shape, scale):
    return jax.random.normal(k, shape, jnp.float32) * scale


def setup_inputs(seed: int = 0) -> dict:
    key = jax.random.key(seed)
    ks = jax.random.split(key, 32)
    D, F = D_MODEL, D_FF
    buf_lens = [min(w, PAST_LEN) for (w, _) in DILATED_GROUPS]
    return {
        'x_prompt': _normal(ks[0], (BATCH, SEQ, D), 1.0),
        'x_sample': _normal(ks[1], (DEC_BATCH, DEC_SEQ, D), 1.0),
        'c_prompt': _normal(ks[2], (BATCH, D), 1.0),
        'c_sample': _normal(ks[3], (DEC_BATCH, D), 1.0),
        'cache_kv_g1': _normal(ks[4], (N_ATTN_LAYERS, DEC_BATCH, buf_lens[0], 2, HEADS_PER_GROUP, HEAD_DIM), 1.0),
        'cache_kv_g2': _normal(ks[5], (N_ATTN_LAYERS, DEC_BATCH, buf_lens[1], 2, HEADS_PER_GROUP, HEAD_DIM), 1.0),
        'cache_kv_g3': _normal(ks[6], (N_ATTN_LAYERS, DEC_BATCH, buf_lens[2], 2, HEADS_PER_GROUP, HEAD_DIM), 1.0),
        'state_conv': _normal(ks[7], (N_CONV_LAYERS, DEC_BATCH, CONF_KERNEL - 1, D), 0.5),
        'state_ffn_conv': _normal(ks[8], (DEPTH, DEC_BATCH, FFN_KERNEL - 1, 2 * F), 1.0),
        'w_ada': _normal(ks[9], (DEPTH, D, 6 * D), D ** -0.5),
        'b_ada': _normal(ks[10], (DEPTH, 6 * D), 0.02),
        'w_qkv': _normal(ks[11], (N_ATTN_LAYERS, D, QKV_WIDTH), D ** -0.5),
        'w_o': _normal(ks[12], (N_ATTN_LAYERS, ATTN_WIDTH, D), ATTN_WIDTH ** -0.5),
        'w_pw1': _normal(ks[13], (N_CONV_LAYERS, D, 2 * D), D ** -0.5),
        'b_pw1': _normal(ks[14], (N_CONV_LAYERS, 2 * D), 0.02),
        'w_dw': _normal(ks[15], (N_CONV_LAYERS, CONF_KERNEL, D), CONF_KERNEL ** -0.5),
        'b_dw': _normal(ks[16], (N_CONV_LAYERS, D), 0.02),
        'ln_g': 1.0 + _normal(ks[17], (N_CONV_LAYERS, D), 0.02),
        'ln_b': _normal(ks[18], (N_CONV_LAYERS, D), 0.02),
        'w_pw2': _normal(ks[19], (N_CONV_LAYERS, D, D), D ** -0.5),
        'b_pw2': _normal(ks[20], (N_CONV_LAYERS, D), 0.02),
        'w_up': _normal(ks[21], (DEPTH, D, 2 * F), D ** -0.5),
        'w_ffn_dw': _normal(ks[22], (DEPTH, FFN_KERNEL, 2 * F), FFN_KERNEL ** -0.5),
        'w_down': _normal(ks[23], (DEPTH, F, D), F ** -0.5),
        'final_norm_g': 1.0 + _normal(ks[24], (D,), 0.02),
    }


def reference(x_prompt, x_sample, c_prompt, c_sample, cache_kv_g1, cache_kv_g2, cache_kv_g3,
              state_conv, state_ffn_conv, w_ada, b_ada, w_qkv, w_o, w_pw1, b_pw1, w_dw, b_dw,
              ln_g, ln_b, w_pw2, b_pw2, w_up, w_ffn_dw, w_down, final_norm_g):
    params = (w_ada, b_ada, w_qkv, w_o, w_pw1, b_pw1, w_dw, b_dw, ln_g, ln_b, w_pw2, b_pw2,
              w_up, w_ffn_dw, w_down, final_norm_g)
    y_prompt, kv_p, conv_prompt, ffn_conv_prompt = trunk(x_prompt, c_prompt, None, None, None, params)
    y_sample, kv_s, conv_sample, ffn_conv_sample = trunk(
        x_sample, c_sample, (cache_kv_g1, cache_kv_g2, cache_kv_g3), state_conv, state_ffn_conv, params)
    kv_g1_prompt, kv_g2_prompt, kv_g3_prompt = kv_p
    kv_g1_sample, kv_g2_sample, kv_g3_sample = kv_s
    return (y_prompt, y_sample, kv_g1_prompt, kv_g2_prompt, kv_g3_prompt, conv_prompt, ffn_conv_prompt,
            kv_g1_sample, kv_g2_sample, kv_g3_sample, conv_sample, ffn_conv_sample)
```

```python
import functools

import jax
import jax.numpy as jnp
from jax import lax
from jax.experimental import pallas as pl
from jax.experimental.pallas import tpu as pltpu

F32 = jnp.float32
BF16 = jnp.bfloat16

HEAD_DIM = 128
Q_BLOCK = 128
DILATED_GROUPS = ((128, 1), (512, 4), (2048, 16))
N_GROUPS = len(DILATED_GROUPS)
RMS_EPS = 1e-6
LN_EPS = 1e-5
ADA_ROWS = 16
CONV_CTX_ROWS = 32
V7X_VMEM_LIMIT_BYTES = 56 * 1024 * 1024


def _params(*sem):
    return pltpu.CompilerParams(dimension_semantics=sem, vmem_limit_bytes=V7X_VMEM_LIMIT_BYTES)


def _col_tile(n, pref):
    if n <= pref:
        return n
    t = pref - pref % 128
    while t >= 128:
        if n % t == 0:
            return t
        t -= 128
    raise ValueError(f"no lane-aligned tile for {n}")


def _row_tile(m, pref):
    t = 1
    while t * 2 <= pref and m % (t * 2) == 0:
        t *= 2
    return t


class _Mod:
    def __init__(self, arr, per_row, rows_per_seq):
        self.arr, self.per_row, self.rs = arr, per_row, rows_per_seq


def _mod_value(ref, m, tm, rs, per_row):
    if per_row:
        return ref[...]
    return ref[pl.ds((m * tm) // rs, 1), :]


def _adaln_kernel(c_ref, w_ref, b_ref, o_ref):
    c = c_ref[...]
    s = (c * jax.nn.sigmoid(c)).astype(BF16)
    o_ref[...] = jnp.dot(s, w_ref[...].astype(BF16), preferred_element_type=F32) + b_ref[...]


def _adaln(c_rows, w_ada, b_ada):
    depth, d, n = w_ada.shape
    tn = _col_tile(n, 512)
    return pl.pallas_call(
        _adaln_kernel,
        grid=(depth, n // tn),
        in_specs=[pl.BlockSpec((ADA_ROWS, d), lambda i, j: (0, 0)),
                  pl.BlockSpec((None, d, tn), lambda i, j: (i, 0, j)),
                  pl.BlockSpec((None, 1, tn), lambda i, j: (i, 0, j))],
        out_specs=pl.BlockSpec((None, ADA_ROWS, tn), lambda i, j: (i, 0, j)),
        out_shape=jax.ShapeDtypeStruct((depth, ADA_ROWS, n), F32),
        compiler_params=_params("arbitrary", "arbitrary"),
        name="adaln",
    )(c_rows, w_ada, b_ada.reshape(depth, 1, n))


def _modulate_kernel(x_ref, sh_ref, sc_ref, o_ref, *, tm, rs, per_row):
    m = pl.program_id(0)
    x = x_ref[...]
    r = lax.rsqrt(jnp.mean(x * x, axis=-1, keepdims=True) + RMS_EPS)
    sh = _mod_value(sh_ref, m, tm, rs, per_row)
    sc = _mod_value(sc_ref, m, tm, rs, per_row)
    o_ref[...] = ((x * r) * (1.0 + sc) + sh).astype(o_ref.dtype)


def _modulate(x, mod, layer, shift_chunk):
    mrows, d = x.shape
    tm = _row_tile(mrows, 256)
    if mod.per_row:
        spec = lambda c: pl.BlockSpec((None, tm, d), lambda m: (layer, m, c))
    else:
        spec = lambda c: pl.BlockSpec((None, ADA_ROWS, d), lambda m: (layer, 0, c))
    return pl.pallas_call(
        functools.partial(_modulate_kernel, tm=tm, rs=mod.rs, per_row=mod.per_row),
        grid=(mrows // tm,),
        in_specs=[pl.BlockSpec((tm, d), lambda m: (m, 0)), spec(shift_chunk), spec(shift_chunk + 1)],
        out_specs=pl.BlockSpec((tm, d), lambda m: (m, 0)),
        out_shape=jax.ShapeDtypeStruct((mrows, d), BF16),
        compiler_params=_params("arbitrary"),
        name="modulate",
    )(x, mod.arr, mod.arr)


def _final_norm_kernel(x_ref, g_ref, o_ref):
    x = x_ref[...]
    r = lax.rsqrt(jnp.mean(x * x, axis=-1, keepdims=True) + RMS_EPS)
    o_ref[...] = (x * r) * g_ref[...]


def _final_norm(x, g):
    mrows, d = x.shape
    tm = _row_tile(mrows, 256)
    return pl.pallas_call(
        _final_norm_kernel,
        grid=(mrows // tm,),
        in_specs=[pl.BlockSpec((tm, d), lambda m: (m, 0)), pl.BlockSpec((1, d), lambda m: (0, 0))],
        out_specs=pl.BlockSpec((tm, d), lambda m: (m, 0)),
        out_shape=jax.ShapeDtypeStruct((mrows, d), F32),
        compiler_params=_params("arbitrary"),
        name="final_norm",
    )(x, g.reshape(1, d))


def _mm_kernel(*refs, has_bias, has_res, tm, rs, per_row):
    it = iter(refs)
    x_ref, w_ref = next(it), next(it)
    b_ref = next(it) if has_bias else None
    res_ref, g_ref = (next(it), next(it)) if has_res else (None, None)
    o_ref, wbf_ref = next(it), next(it)
    m = pl.program_id(1)

    @pl.when(m == 0)
    def _():
        wbf_ref[...] = w_ref[...].astype(BF16)

    acc = jnp.dot(x_ref[...], wbf_ref[...], preferred_element_type=F32)
    if has_bias:
        acc = acc + b_ref[...]
    if has_res:
        acc = res_ref[...] + _mod_value(g_ref, m, tm, rs, per_row) * acc
    o_ref[...] = acc.astype(o_ref.dtype)


def _mm(x, w, layer, *, out_dtype, bias=None, res=None, gate=None, tm_pref=1024, tn_pref=512):
    mrows, k = x.shape
    n = w.shape[2]
    tm = _row_tile(mrows, tm_pref)
    tn = _col_tile(n, tn_pref)
    in_specs = [pl.BlockSpec((tm, k), lambda j, m: (m, 0)),
                pl.BlockSpec((None, k, tn), lambda j, m: (layer, 0, j))]
    args = [x, w]
    if bias is not None:
        in_specs.append(pl.BlockSpec((None, 1, tn), lambda j, m: (layer, 0, j)))
        args.append(bias.reshape(bias.shape[0], 1, n))
    rs, per_row = 1, True
    if res is not None:
        mod, gl, gate_chunk = gate
        rs, per_row = mod.rs, mod.per_row
        goff = gate_chunk * (n // tn)
        in_specs.append(pl.BlockSpec((tm, tn), lambda j, m: (m, j)))
        if per_row:
            in_specs.append(pl.BlockSpec((None, tm, tn), lambda j, m: (gl, m, goff + j)))
        else:
            in_specs.append(pl.BlockSpec((None, ADA_ROWS, tn), lambda j, m: (gl, 0, goff + j)))
        args += [res, mod.arr]
    return pl.pallas_call(
        functools.partial(_mm_kernel, has_bias=bias is not None, has_res=res is not None,
                          tm=tm, rs=rs, per_row=per_row),
        grid=(n // tn, mrows // tm),
        in_specs=in_specs,
        out_specs=pl.BlockSpec((tm, tn), lambda j, m: (m, j)),
        out_shape=jax.ShapeDtypeStruct((mrows, n), out_dtype),
        scratch_shapes=[pltpu.VMEM((k, tn), BF16)],
        compiler_params=_params("arbitrary", "arbitrary"),
        name="matmul",
    )(*args)


def _causal_taps(u, dw_ref, row, p0, p1):
    u1 = jnp.where(row == 0, p1, pltpu.roll(u, 1, 0))
    u2 = jnp.where(row == 0, p0, jnp.where(row == 1, p1, pltpu.roll(u, 2, 0)))
    return dw_ref[0:1, :] * u2 + dw_ref[1:2, :] * u1 + dw_ref[2:3, :] * u


def _pair_kernel(*refs, mode, carry, tm, rs):
    it = iter(refs)
    x_ref, wa_ref, wb_ref = next(it), next(it), next(it)
    if mode == "glu":
        ba_ref, bb_ref = next(it), next(it)
    else:
        dwa_ref, dwb_ref = next(it), next(it)
        if not carry:
            c0a_ref, c0b_ref, c1a_ref, c1b_ref = next(it), next(it), next(it), next(it)
    o_ref = next(it)
    if mode == "ffn":
        ta_ref, tb_ref = next(it), next(it)
    wabf_ref, wbbf_ref = next(it), next(it)
    if mode == "ffn" and carry:
        ca_ref, cb_ref = next(it), next(it)
    m = pl.program_id(1)

    @pl.when(m == 0)
    def _():
        wabf_ref[...] = wa_ref[...].astype(BF16)
        wbbf_ref[...] = wb_ref[...].astype(BF16)

    x = x_ref[...]
    ua = jnp.dot(x, wabf_ref[...], preferred_element_type=F32)
    ub = jnp.dot(x, wbbf_ref[...], preferred_element_type=F32)

    if mode == "glu":
        o_ref[...] = ((ua + ba_ref[...]) * jax.nn.sigmoid(ub + bb_ref[...])).astype(o_ref.dtype)
        return

    row = lax.broadcasted_iota(jnp.int32, ua.shape, 0)
    if carry:
        @pl.when((m * tm) % rs == 0)
        def _():
            ca_ref[...] = jnp.zeros_like(ca_ref)
            cb_ref[...] = jnp.zeros_like(cb_ref)

        gate = _causal_taps(ua, dwa_ref, row, ca_ref[6:7, :], ca_ref[7:8, :])
        val = _causal_taps(ub, dwb_ref, row, cb_ref[6:7, :], cb_ref[7:8, :])
        ca_ref[...] = ua[tm - 8:tm, :]
        cb_ref[...] = ub[tm - 8:tm, :]

        @pl.when(((m + 1) * tm) % rs == 0)
        def _():
            ta_ref[...] = ua[tm - 8:tm, :]
            tb_ref[...] = ub[tm - 8:tm, :]
    else:
        assert rs & (rs - 1) == 0
        row = row & (rs - 1)
        gate = _causal_taps(ua, dwa_ref, row, c0a_ref[...], c1a_ref[...])
        val = _causal_taps(ub, dwb_ref, row, c0b_ref[...], c1b_ref[...])
        ta_ref[...] = ua
        tb_ref[...] = ub
    o_ref[...] = ((gate * jax.nn.sigmoid(gate)) * val).astype(o_ref.dtype)


def _pair_mm(x, w, layer, *, mode, rs, bias=None, dw=None, ctx=None, tm_pref=1024, tn_pref=256):
    mrows, k = x.shape
    n2 = w.shape[2]
    half = n2 // 2
    tm = _row_tile(mrows, tm_pref)
    tn = _col_tile(half, tn_pref)
    hoff = half // tn
    carry = mode == "ffn" and ctx is None
    in_specs = [pl.BlockSpec((tm, k), lambda j, m: (m, 0)),
                pl.BlockSpec((None, k, tn), lambda j, m: (layer, 0, j)),
                pl.BlockSpec((None, k, tn), lambda j, m: (layer, 0, hoff + j))]
    args = [x, w, w]
    scratch = [pltpu.VMEM((k, tn), BF16), pltpu.VMEM((k, tn), BF16)]
    if mode == "glu":
        b3 = bias.reshape(bias.shape[0], 1, n2)
        in_specs += [pl.BlockSpec((None, 1, tn), lambda j, m: (layer, 0, j)),
                     pl.BlockSpec((None, 1, tn), lambda j, m: (layer, 0, hoff + j))]
        args += [b3, b3]
        out_shape = jax.ShapeDtypeStruct((mrows, half), F32)
        out_specs = pl.BlockSpec((tm, tn), lambda j, m: (m, j))
    else:
        taps = dw.shape[1]
        in_specs += [pl.BlockSpec((None, taps, tn), lambda j, m: (layer, 0, j)),
                     pl.BlockSpec((None, taps, tn), lambda j, m: (layer, 0, hoff + j))]
        args += [dw, dw]
        act_shape = jax.ShapeDtypeStruct((mrows, half), BF16)
        act_spec = pl.BlockSpec((tm, tn), lambda j, m: (m, j))
        if carry:
            nseq = mrows // rs
            tail_shape = jax.ShapeDtypeStruct((nseq, 8, half), F32)
            tail_spec = pl.BlockSpec((None, 8, tn), lambda j, m: ((m * tm) // rs, 0, j))
            scratch += [pltpu.VMEM((8, tn), F32), pltpu.VMEM((8, tn), F32)]
        else:
            c0, c1 = ctx
            lo = pl.BlockSpec((tm, tn), lambda j, m: (m, j))
            hi = pl.BlockSpec((tm, tn), lambda j, m: (m, hoff + j))
            in_specs += [lo, hi, lo, hi]
            args += [c0, c0, c1, c1]
            tail_shape = jax.ShapeDtypeStruct((mrows, half), F32)
            tail_spec = pl.BlockSpec((tm, tn), lambda j, m: (m, j))
        out_shape = (act_shape, tail_shape, tail_shape)
        out_specs = (act_spec, tail_spec, tail_spec)
    return pl.pallas_call(
        functools.partial(_pair_kernel, mode=mode, carry=carry, tm=tm, rs=rs),
        grid=(half // tn, mrows // tm),
        in_specs=in_specs,
        out_specs=out_specs,
        out_shape=out_shape,
        scratch_shapes=scratch,
        compiler_params=_params("arbitrary", "arbitrary"),
        name="pair_" + mode,
    )(*args)


def _attn_prompt_kernel(q_ref, kp_ref, kc_ref, vp_ref, vc_ref, o_ref, l_ref, *, heads, n_steps):
    lblk = pl.program_id(2)
    qi = lax.broadcasted_iota(jnp.int32, (Q_BLOCK, 2 * Q_BLOCK), 0)
    ki = lax.broadcasted_iota(jnp.int32, (Q_BLOCK, 2 * Q_BLOCK), 1)
    dist = qi + Q_BLOCK - ki
    key_idx = lblk * Q_BLOCK - Q_BLOCK + ki
    valid = (dist >= 0) & (dist <= n_steps) & (key_idx >= 0)
    bias = jnp.where(valid, 0.0, -jnp.inf).astype(F32)
    scale = HEAD_DIM ** -0.5
    for h in range(heads):
        sl = slice(h * HEAD_DIM, (h + 1) * HEAD_DIM)
        q = q_ref[:, sl].astype(BF16)
        k = jnp.concatenate([kp_ref[:, sl], kc_ref[:, sl]], axis=0).astype(BF16)
        v = jnp.concatenate([vp_ref[:, sl], vc_ref[:, sl]], axis=0).astype(BF16)
        s = lax.dot_general(q, k, (((1,), (1,)), ((), ())), preferred_element_type=F32) * scale + bias
        mx = jnp.max(s, axis=-1, keepdims=True)
        p = jnp.exp(s - mx)
        den = jnp.sum(p, axis=-1, keepdims=True)
        o_ref[:, sl] = jnp.dot(p.astype(BF16), v, preferred_element_type=F32) / den
        l_ref[:, sl] = jnp.broadcast_to(mx + jnp.log(den), (Q_BLOCK, HEAD_DIM))


def _attn_prompt(qkv, nseq, seq, g):
    window, dil = DILATED_GROUPS[g]
    n_steps = window // dil
    aw = qkv.shape[1] // (3 * N_GROUPS)
    heads = aw // HEAD_DIM
    sub = seq // dil
    assert sub % Q_BLOCK == 0 and n_steps <= Q_BLOCK
    view = qkv.reshape(nseq, sub, dil * 3 * N_GROUPS * aw)
    cpb = 3 * N_GROUPS
    blk = (None, Q_BLOCK, aw)

    def cur(t):
        return pl.BlockSpec(blk, lambda b, r, l: (b, l, r * cpb + g * 3 + t))

    def prev(t):
        return pl.BlockSpec(blk, lambda b, r, l: (b, jnp.maximum(l - 1, 0), r * cpb + g * 3 + t))

    out_spec = pl.BlockSpec(blk, lambda b, r, l: (b, l, r))
    out_shape = jax.ShapeDtypeStruct((nseq, sub, dil * aw), F32)
    o, lse = pl.pallas_call(
        functools.partial(_attn_prompt_kernel, heads=heads, n_steps=n_steps),
        grid=(nseq, dil, sub // Q_BLOCK),
        in_specs=[cur(0), prev(1), cur(1), prev(2), cur(2)],
        out_specs=(out_spec, out_spec),
        out_shape=(out_shape, out_shape),
        compiler_params=_params("arbitrary", "arbitrary", "arbitrary"),
        name=f"attn_prompt_g{g}",
    )(view, view, view, view, view)
    return o.reshape(nseq * seq, aw), lse.reshape(nseq * seq, aw)


def _combine_kernel(o1_ref, o2_ref, o3_ref, l1_ref, l2_ref, l3_ref, out_ref):
    l1, l2, l3 = l1_ref[...], l2_ref[...], l3_ref[...]
    mx = jnp.maximum(jnp.maximum(l1, l2), l3)
    e1, e2, e3 = jnp.exp(l1 - mx), jnp.exp(l2 - mx), jnp.exp(l3 - mx)
    num = e1 * o1_ref[...] + e2 * o2_ref[...] + e3 * o3_ref[...]
    out_ref[...] = (num / (e1 + e2 + e3)).astype(out_ref.dtype)


def _combine(outs, lses):
    mrows, aw = outs[0].shape
    tm = _row_tile(mrows, 256)
    spec = pl.BlockSpec((tm, aw), lambda m: (m, 0))
    return pl.pallas_call(
        _combine_kernel,
        grid=(mrows // tm,),
        in_specs=[spec] * 6,
        out_specs=spec,
        out_shape=jax.ShapeDtypeStruct((mrows, aw), BF16),
        compiler_params=_params("arbitrary"),
        name="attn_combine",
    )(*outs, *lses)


def _attn_sample_kernel(qkv_ref, c1_ref, c2_ref, c3_ref, o_ref, *, heads, steps):
    t = pl.program_id(1)
    aw = heads * HEAD_DIM
    scale = HEAD_DIM ** -0.5
    nt = (((1,), (1,)), ((), ()))
    head_of_col = lax.shift_right_logical(lax.broadcasted_iota(jnp.int32, (heads, aw), 1),
                                          jnp.int32(HEAD_DIM.bit_length() - 1))
    head_row = lax.broadcasted_iota(jnp.int32, (heads, aw), 0)
    diag = head_of_col == head_row
    cache_pos = lax.broadcasted_iota(jnp.int32, (heads, Q_BLOCK), 1)
    new_pos = lax.broadcasted_iota(jnp.int32, (heads, steps), 1)
    scores, values = [], []
    for g, c_ref in enumerate((c1_ref, c2_ref, c3_ref)):
        _, dil = DILATED_GROUPS[g]
        base = g * 3 * aw
        q = qkv_ref[pl.ds(t, 1), base:base + aw]
        qbd = jnp.where(diag, jnp.broadcast_to(q, (heads, aw)), 0.0).astype(BF16)
        kc = c_ref[:, 0:aw].astype(BF16)
        kn = qkv_ref[:, base + aw:base + 2 * aw].astype(BF16)
        sc = lax.dot_general(qbd, kc, nt, preferred_element_type=F32) * scale
        sn = lax.dot_general(qbd, kn, nt, preferred_element_type=F32) * scale
        sc = jnp.where(cache_pos >= t // dil, sc, -jnp.inf)
        back = t - new_pos
        sn = jnp.where((back >= 0) & ((back & (dil - 1)) == 0), sn, -jnp.inf)
        scores += [sc, sn]
        values += [c_ref[:, aw:2 * aw].astype(BF16), qkv_ref[:, base + 2 * aw:base + 3 * aw].astype(BF16)]
    mx = functools.reduce(jnp.maximum, [jnp.max(s, axis=-1, keepdims=True) for s in scores])
    den = jnp.zeros((heads, 1), F32)
    acc = jnp.zeros((heads, aw), F32)
    for s, v in zip(scores, values):
        p = jnp.exp(s - mx)
        den = den + jnp.sum(p, axis=-1, keepdims=True)
        acc = acc + jnp.dot(p.astype(BF16), v, preferred_element_type=F32)
    o_ref[pl.ds(t, 1), :] = jnp.sum(jnp.where(diag, acc / den, 0.0), axis=0, keepdims=True)


def _attn_sample(qkv, caches, layer, nseq, steps):
    aw = qkv.shape[1] // (3 * N_GROUPS)
    heads = aw // HEAD_DIM
    in_specs = [pl.BlockSpec((None, steps, 3 * N_GROUPS * aw), lambda b, t: (b, 0, 0))]
    args = [qkv.reshape(nseq, steps, 3 * N_GROUPS * aw)]
    for (window, dil), cache in zip(DILATED_GROUPS, caches):
        assert cache.shape[2] == window and window // dil == Q_BLOCK
        args.append(cache.reshape(cache.shape[0] * nseq, window // dil, dil * 2 * aw))
        in_specs.append(pl.BlockSpec((None, Q_BLOCK, 2 * aw),
                                     lambda b, t, dil=dil: (layer * nseq + b, 0, t % dil)))
    out = pl.pallas_call(
        functools.partial(_attn_sample_kernel, heads=heads, steps=steps),
        grid=(nseq, steps),
        in_specs=in_specs,
        out_specs=pl.BlockSpec((None, steps, aw), lambda b, t: (b, 0, 0)),
        out_shape=jax.ShapeDtypeStruct((nseq, steps, aw), F32),
        compiler_params=_params("arbitrary", "arbitrary"),
        name="attn_sample",
    )(*args)
    return out.reshape(nseq * steps, aw)


def _slide_kernel(cache_ref, new_ref, out_ref, sems, *, layer, nseq, steps, window):
    keep = window - steps
    old = pltpu.make_async_copy(cache_ref.at[pl.ds(layer * nseq, nseq), pl.ds(steps, keep), :],
                                out_ref.at[:, pl.ds(0, keep), :], sems.at[0])
    new = pltpu.make_async_copy(new_ref, out_ref.at[:, pl.ds(keep, steps), :], sems.at[1])
    old.start()
    new.start()
    old.wait()
    new.wait()


def _slide_cache(cache, layer, new_rows):
    layers, nseq, window = cache.shape[:3]
    steps, width = new_rows.shape[1:]
    out = pl.pallas_call(
        functools.partial(_slide_kernel, layer=layer, nseq=nseq, steps=steps, window=window),
        in_specs=[pl.BlockSpec(memory_space=pl.ANY), pl.BlockSpec(memory_space=pl.ANY)],
        out_specs=pl.BlockSpec(memory_space=pl.ANY),
        out_shape=jax.ShapeDtypeStruct((nseq, window, width), F32),
        scratch_shapes=[pltpu.SemaphoreType.DMA((2,))],
        name="cache_slide",
    )(cache.reshape(layers * nseq, window, width), new_rows)
    return out.reshape((nseq,) + cache.shape[2:])


_CONV_ROWS = 32
_CONV_COLS = 512


def _dwln_kernel(u_ref, ctx_ref, w_ref, bdw_ref, g_ref, b_ref, o_ref, ext_ref, y_ref, *, tm, rs, taps, zero_start):
    m = pl.program_id(0)
    if zero_start:
        at_start = (m * tm) % rs == 0

        @pl.when(at_start)
        def _():
            ext_ref[0:CONV_CTX_ROWS, :] = jnp.zeros((CONV_CTX_ROWS, ext_ref.shape[1]), F32)

        @pl.when(jnp.logical_not(at_start))
        def _():
            ext_ref[0:CONV_CTX_ROWS, :] = ctx_ref[...]
    else:
        ext_ref[0:CONV_CTX_ROWS, :] = ctx_ref[...]
    ext_ref[CONV_CTX_ROWS:CONV_CTX_ROWS + tm, :] = u_ref[...]

    d = u_ref.shape[1]
    rc = min(_CONV_ROWS, tm)
    cw = min(_CONV_COLS, d)
    first = CONV_CTX_ROWS - (taps - 1)
    for r0 in range(0, tm, rc):
        for c0 in range(0, d, cw):
            acc = jnp.zeros((rc, cw), F32)
            for k in range(taps):
                acc = acc + w_ref[k:k + 1, c0:c0 + cw] * ext_ref[r0 + first + k:r0 + first + k + rc, c0:c0 + cw]
            y_ref[r0:r0 + rc, c0:c0 + cw] = acc + bdw_ref[:, c0:c0 + cw]

    y = y_ref[...]
    mu = jnp.mean(y, axis=-1, keepdims=True)
    yc = y - mu
    var = jnp.mean(yc * yc, axis=-1, keepdims=True)
    z = yc * lax.rsqrt(var + LN_EPS) * g_ref[...] + b_ref[...]
    o_ref[...] = (z * jax.nn.sigmoid(z)).astype(o_ref.dtype)


def _dwln(u, ctx, layer, w_dw, b_dw, ln_g, ln_b, *, rs, tm):
    mrows, d = u.shape
    taps = w_dw.shape[1]
    assert taps - 1 <= CONV_CTX_ROWS and tm % 8 == 0
    wpad = jnp.pad(w_dw, ((0, 0), (0, CONV_CTX_ROWS - taps), (0, 0)))
    zero_start = ctx is None
    if zero_start:
        assert tm % CONV_CTX_ROWS == 0
        per = tm // CONV_CTX_ROWS
        ctx_arr = u
        ctx_spec = pl.BlockSpec((CONV_CTX_ROWS, d), lambda m: (jnp.maximum(m * per - 1, 0), 0))
    else:
        assert tm == rs
        ctx_arr = ctx
        ctx_spec = pl.BlockSpec((None, CONV_CTX_ROWS, d), lambda m: (m, 0, 0))
    vec = lambda a: a.reshape(a.shape[0], 1, d)
    vec_spec = pl.BlockSpec((None, 1, d), lambda m: (layer, 0, 0))
    return pl.pallas_call(
        functools.partial(_dwln_kernel, tm=tm, rs=rs, taps=taps, zero_start=zero_start),
        grid=(mrows // tm,),
        in_specs=[pl.BlockSpec((tm, d), lambda m: (m, 0)), ctx_spec,
                  pl.BlockSpec((None, CONV_CTX_ROWS, d), lambda m: (layer, 0, 0)),
                  vec_spec, vec_spec, vec_spec],
        out_specs=pl.BlockSpec((tm, d), lambda m: (m, 0)),
        out_shape=jax.ShapeDtypeStruct((mrows, d), BF16),
        scratch_shapes=[pltpu.VMEM((CONV_CTX_ROWS + tm, d), F32), pltpu.VMEM((tm, d), F32)],
        compiler_params=_params("arbitrary"),
        name="dwconv_ln",
    )(u, ctx_arr, wpad, vec(b_dw), vec(ln_g), vec(ln_b))


def _trunk(x, mod, nseq, rs, params, kv_caches, conv_states, ffn_states):
    (w_qkv, w_o, w_pw1, b_pw1, w_dw, b_dw, ln_g, ln_b, w_pw2, b_pw2,
     w_up, w_ffn_dw, w_down, final_norm_g) = params
    depth = w_up.shape[0]
    d = x.shape[1]
    prompt = kv_caches is None
    new_kv = [[] for _ in range(N_GROUPS)]
    new_conv, new_ffn = [], []
    for i in range(depth):
        j = i // 2
        h = _modulate(x, mod, i, 0)
        if i % 2 == 0:
            qkv = _mm(h, w_qkv, j, out_dtype=F32)
            aw = qkv.shape[1] // (3 * N_GROUPS)
            heads = aw // HEAD_DIM
            qkv3 = qkv.reshape(nseq, rs, 3 * N_GROUPS * aw)
            if prompt:
                outs, lses = zip(*[_attn_prompt(qkv, nseq, rs, g) for g in range(N_GROUPS)])
                mixed = _combine(outs, lses)
                for g, (window, _) in enumerate(DILATED_GROUPS):
                    keep = min(window, rs)
                    kv = qkv3[:, rs - keep:, g * 3 * aw + aw:(g + 1) * 3 * aw]
                    new_kv[g].append(kv.reshape(nseq, keep, 2, heads, HEAD_DIM))
            else:
                mixed = _attn_sample(qkv, kv_caches, j, nseq, rs).astype(BF16)
                for g in range(N_GROUPS):
                    rows = qkv3[:, :, g * 3 * aw + aw:(g + 1) * 3 * aw]
                    new_kv[g].append(_slide_cache(kv_caches[g], j, rows))
            x = _mm(mixed, w_o, j, out_dtype=F32, res=x, gate=(mod, i, 2))
        else:
            u = _pair_mm(h, w_pw1, j, mode="glu", rs=rs, bias=b_pw1)
            taps = w_dw.shape[1]
            if prompt:
                y = _dwln(u, None, j, w_dw, b_dw, ln_g, ln_b, rs=rs, tm=128)
                new_conv.append(u.reshape(nseq, rs, d)[:, rs - (taps - 1):])
            else:
                state = conv_states[j]
                ctx = jnp.pad(state, ((0, 0), (CONV_CTX_ROWS - (taps - 1), 0), (0, 0)))
                y = _dwln(u, ctx, j, w_dw, b_dw, ln_g, ln_b, rs=rs, tm=rs)
                u_all = jnp.concatenate([state, u.reshape(nseq, rs, d)], axis=1)
                new_conv.append(u_all[:, -(taps - 1):])
            x = _mm(y, w_pw2, j, out_dtype=F32, bias=b_pw2, res=x, gate=(mod, i, 2))
        h = _modulate(x, mod, i, 3)
        if prompt:
            act, ta, tb = _pair_mm(h, w_up, i, mode="ffn", rs=rs, dw=w_ffn_dw)
            new_ffn.append(jnp.concatenate([ta[:, 6:8], tb[:, 6:8]], axis=-1))
        else:
            state = ffn_states[i]
            c0 = jnp.repeat(state[:, 0], rs, axis=0)
            c1 = jnp.repeat(state[:, 1], rs, axis=0)
            act, ta, tb = _pair_mm(h, w_up, i, mode="ffn", rs=rs, dw=w_ffn_dw, ctx=(c0, c1))
            u_ffn = jnp.concatenate([ta, tb], axis=-1).reshape(nseq, rs, -1)
            new_ffn.append(jnp.concatenate([state, u_ffn], axis=1)[:, -state.shape[1]:])
        x = _mm(act, w_down, i, out_dtype=F32, res=x, gate=(mod, i, 5), tm_pref=256, tn_pref=256)
    y = _final_norm(x, final_norm_g)
    return y, [jnp.stack(l) for l in new_kv], jnp.stack(new_conv), jnp.stack(new_ffn)


def kernel(x_prompt, x_sample, c_prompt, c_sample, cache_kv_g1, cache_kv_g2, cache_kv_g3, state_conv, state_ffn_conv, w_ada, b_ada, w_qkv, w_o, w_pw1, b_pw1, w_dw, b_dw, ln_g, ln_b, w_pw2, b_pw2, w_up, w_ffn_dw, w_down, final_norm_g):
    nb, seq, d = x_prompt.shape
    ndb, steps, _ = x_sample.shape
    assert nb + ndb <= ADA_ROWS
    params = (w_qkv, w_o, w_pw1, b_pw1, w_dw, b_dw, ln_g, ln_b, w_pw2, b_pw2,
              w_up, w_ffn_dw, w_down, final_norm_g)

    c_rows = jnp.concatenate([c_prompt, c_sample, jnp.zeros((ADA_ROWS - nb - ndb, d), F32)], axis=0)
    ada = _adaln(c_rows, w_ada, b_ada)
    mod_p = _Mod(ada, False, seq)
    mod_s = _Mod(jnp.repeat(ada[:, nb:nb + ndb], steps, axis=1), True, steps)

    y_p, kv_p, conv_p, ffn_p = _trunk(x_prompt.reshape(nb * seq, d), mod_p, nb, seq, params, None, None, None)
    y_s, kv_s, conv_s, ffn_s = _trunk(x_sample.reshape(ndb * steps, d), mod_s, ndb, steps, params,
                                      (cache_kv_g1, cache_kv_g2, cache_kv_g3), state_conv, state_ffn_conv)
    return (y_p.reshape(nb, seq, d), y_s.reshape(ndb, steps, d), kv_p[0], kv_p[1], kv_p[2], conv_p, ffn_p,
            kv_s[0], kv_s[1], kv_s[2], conv_s, ffn_s)
```

```python
import functools

import jax
import jax.numpy as jnp
from jax import lax
from jax.experimental import pallas as pl
from jax.experimental.pallas import tpu as pltpu

F32 = jnp.float32
BF16 = jnp.bfloat16

HEAD_DIM = 128
Q_BLOCK = 128
DILATED_GROUPS = ((128, 1), (512, 4), (2048, 16))
N_GROUPS = len(DILATED_GROUPS)
RMS_EPS = 1e-6
LN_EPS = 1e-5
ADA_ROWS = 16
CONV_CTX_ROWS = 32
V7X_VMEM_LIMIT_BYTES = 56 * 1024 * 1024


def _params(*sem):
    return pltpu.CompilerParams(dimension_semantics=sem, vmem_limit_bytes=V7X_VMEM_LIMIT_BYTES)


def _col_tile(n, pref):
    if n <= pref:
        return n
    t = pref - pref % 128
    while t >= 128:
        if n % t == 0:
            return t
        t -= 128
    raise ValueError(f"no lane-aligned tile for {n}")


def _row_tile(m, pref):
    t = 1
    while t * 2 <= pref and m % (t * 2) == 0:
        t *= 2
    return t


class _Mod:
    def __init__(self, arr, per_row, rows_per_seq):
        self.arr, self.per_row, self.rs = arr, per_row, rows_per_seq


def _mod_value(ref, m, tm, rs, per_row):
    if per_row:
        return ref[...]
    return ref[pl.ds((m * tm) // rs, 1), :]


def _adaln_kernel(c_ref, w_ref, b_ref, o_ref):
    c = c_ref[...]
    s = (c * jax.nn.sigmoid(c)).astype(BF16)
    o_ref[...] = jnp.dot(s, w_ref[...].astype(BF16), preferred_element_type=F32) + b_ref[...]


def _adaln(c_rows, w_ada, b_ada):
    depth, d, n = w_ada.shape
    tn = _col_tile(n, 512)
    return pl.pallas_call(
        _adaln_kernel,
        grid=(depth, n // tn),
        in_specs=[pl.BlockSpec((ADA_ROWS, d), lambda i, j: (0, 0)),
                  pl.BlockSpec((None, d, tn), lambda i, j: (i, 0, j)),
                  pl.BlockSpec((None, 1, tn), lambda i, j: (i, 0, j))],
        out_specs=pl.BlockSpec((None, ADA_ROWS, tn), lambda i, j: (i, 0, j)),
        out_shape=jax.ShapeDtypeStruct((depth, ADA_ROWS, n), F32),
        compiler_params=_params("arbitrary", "arbitrary"),
        name="adaln",
    )(c_rows, w_ada, b_ada.reshape(depth, 1, n))


def _modulate_kernel(x_ref, sh_ref, sc_ref, o_ref, *, tm, rs, per_row):
    m = pl.program_id(0)
    x = x_ref[...]
    r = lax.rsqrt(jnp.mean(x * x, axis=-1, keepdims=True) + RMS_EPS)
    sh = _mod_value(sh_ref, m, tm, rs, per_row)
    sc = _mod_value(sc_ref, m, tm, rs, per_row)
    o_ref[...] = ((x * r) * (1.0 + sc) + sh).astype(o_ref.dtype)


def _modulate(x, mod, layer, shift_chunk):
    mrows, d = x.shape
    tm = _row_tile(mrows, 256)
    if mod.per_row:
        spec = lambda c: pl.BlockSpec((None, tm, d), lambda m: (layer, m, c))
    else:
        spec = lambda c: pl.BlockSpec((None, ADA_ROWS, d), lambda m: (layer, 0, c))
    return pl.pallas_call(
        functools.partial(_modulate_kernel, tm=tm, rs=mod.rs, per_row=mod.per_row),
        grid=(mrows // tm,),
        in_specs=[pl.BlockSpec((tm, d), lambda m: (m, 0)), spec(shift_chunk), spec(shift_chunk + 1)],
        out_specs=pl.BlockSpec((tm, d), lambda m: (m, 0)),
        out_shape=jax.ShapeDtypeStruct((mrows, d), BF16),
        compiler_params=_params("arbitrary"),
        name="modulate",
    )(x, mod.arr, mod.arr)


def _final_norm_kernel(x_ref, g_ref, o_ref):
    x = x_ref[...]
    r = lax.rsqrt(jnp.mean(x * x, axis=-1, keepdims=True) + RMS_EPS)
    o_ref[...] = (x * r) * g_ref[...]


def _final_norm(x, g):
    mrows, d = x.shape
    tm = _row_tile(mrows, 256)
    return pl.pallas_call(
        _final_norm_kernel,
        grid=(mrows // tm,),
        in_specs=[pl.BlockSpec((tm, d), lambda m: (m, 0)), pl.BlockSpec((1, d), lambda m: (0, 0))],
        out_specs=pl.BlockSpec((tm, d), lambda m: (m, 0)),
        out_shape=jax.ShapeDtypeStruct((mrows, d), F32),
        compiler_params=_params("arbitrary"),
        name="final_norm",
    )(x, g.reshape(1, d))


def _mm_kernel(*refs, has_bias, has_res, tm, rs, per_row):
    it = iter(refs)
    x_ref, w_ref = next(it), next(it)
    b_ref = next(it) if has_bias else None
    res_ref, g_ref = (next(it), next(it)) if has_res else (None, None)
    o_ref, wbf_ref = next(it), next(it)
    m = pl.program_id(1)

    @pl.when(m == 0)
    def _():
        wbf_ref[...] = w_ref[...].astype(BF16)

    acc = jnp.dot(x_ref[...], wbf_ref[...], preferred_element_type=F32)
    if has_bias:
        acc = acc + b_ref[...]
    if has_res:
        acc = res_ref[...] + _mod_value(g_ref, m, tm, rs, per_row) * acc
    if len(o_ref.shape) == 3:
        for c in range(o_ref.shape[0]):
            o_ref[c] = acc[:, c * HEAD_DIM:(c + 1) * HEAD_DIM].astype(o_ref.dtype)
    else:
        o_ref[...] = acc.astype(o_ref.dtype)


def _mm(x, w, layer, *, out_dtype, bias=None, res=None, gate=None, tm_pref=1024, tn_pref=512,
        head_major=False, single_buffer_w=False):
    mrows, k = x.shape
    n = w.shape[2]
    tm = _row_tile(mrows, tm_pref)
    tn = _col_tile(n, tn_pref)
    w_mode = dict(pipeline_mode=pl.Buffered(1)) if single_buffer_w else {}
    in_specs = [pl.BlockSpec((tm, k), lambda j, m: (m, 0)),
                pl.BlockSpec((None, k, tn), lambda j, m: (layer, 0, j), **w_mode)]
    args = [x, w]
    if bias is not None:
        in_specs.append(pl.BlockSpec((None, 1, tn), lambda j, m: (layer, 0, j)))
        args.append(bias.reshape(bias.shape[0], 1, n))
    rs, per_row = 1, True
    if res is not None:
        mod, gl, gate_chunk = gate
        rs, per_row = mod.rs, mod.per_row
        goff = gate_chunk * (n // tn)
        in_specs.append(pl.BlockSpec((tm, tn), lambda j, m: (m, j)))
        if per_row:
            in_specs.append(pl.BlockSpec((None, tm, tn), lambda j, m: (gl, m, goff + j)))
        else:
            in_specs.append(pl.BlockSpec((None, ADA_ROWS, tn), lambda j, m: (gl, 0, goff + j)))
        args += [res, mod.arr]
    if head_major:
        out_spec = pl.BlockSpec((tn // HEAD_DIM, tm, HEAD_DIM), lambda j, m: (j, m, 0))
        out_shape = jax.ShapeDtypeStruct((n // HEAD_DIM, mrows, HEAD_DIM), out_dtype)
    else:
        out_spec = pl.BlockSpec((tm, tn), lambda j, m: (m, j))
        out_shape = jax.ShapeDtypeStruct((mrows, n), out_dtype)
    return pl.pallas_call(
        functools.partial(_mm_kernel, has_bias=bias is not None, has_res=res is not None,
                          tm=tm, rs=rs, per_row=per_row),
        grid=(n // tn, mrows // tm),
        in_specs=in_specs,
        out_specs=out_spec,
        out_shape=out_shape,
        scratch_shapes=[pltpu.VMEM((k, tn), BF16)],
        compiler_params=_params("arbitrary", "arbitrary"),
        name="matmul",
    )(*args)


def _causal_taps(u, dw_ref, row, p0, p1):
    u1 = jnp.where(row == 0, p1, pltpu.roll(u, 1, 0))
    u2 = jnp.where(row == 0, p0, jnp.where(row == 1, p1, pltpu.roll(u, 2, 0)))
    return dw_ref[0:1, :] * u2 + dw_ref[1:2, :] * u1 + dw_ref[2:3, :] * u


def _pair_kernel(*refs, mode, rs):
    it = iter(refs)
    x_ref, wa_ref, wb_ref = next(it), next(it), next(it)
    if mode == "glu":
        ba_ref, bb_ref = next(it), next(it)
    else:
        dwa_ref, dwb_ref = next(it), next(it)
        c0a_ref, c0b_ref, c1a_ref, c1b_ref = next(it), next(it), next(it), next(it)
    o_ref = next(it)
    if mode == "ffn":
        ta_ref, tb_ref = next(it), next(it)
    wabf_ref, wbbf_ref = next(it), next(it)
    m = pl.program_id(1)

    @pl.when(m == 0)
    def _():
        wabf_ref[...] = wa_ref[...].astype(BF16)
        wbbf_ref[...] = wb_ref[...].astype(BF16)

    x = x_ref[...]
    ua = jnp.dot(x, wabf_ref[...], preferred_element_type=F32)
    ub = jnp.dot(x, wbbf_ref[...], preferred_element_type=F32)

    if mode == "glu":
        o_ref[...] = ((ua + ba_ref[...]) * jax.nn.sigmoid(ub + bb_ref[...])).astype(o_ref.dtype)
        return

    assert rs & (rs - 1) == 0
    row = lax.broadcasted_iota(jnp.int32, ua.shape, 0) & (rs - 1)
    gate = _causal_taps(ua, dwa_ref, row, c0a_ref[...], c1a_ref[...])
    val = _causal_taps(ub, dwb_ref, row, c0b_ref[...], c1b_ref[...])
    ta_ref[...] = ua
    tb_ref[...] = ub
    o_ref[...] = ((gate * jax.nn.sigmoid(gate)) * val).astype(o_ref.dtype)


def _pair_mm(x, w, layer, *, mode, rs, bias=None, dw=None, ctx=None, tm_pref=1024, tn_pref=256):
    mrows, k = x.shape
    n2 = w.shape[2]
    half = n2 // 2
    tm = _row_tile(mrows, tm_pref)
    tn = _col_tile(half, tn_pref)
    hoff = half // tn
    assert mode == "glu" or tm % rs == 0
    in_specs = [pl.BlockSpec((tm, k), lambda j, m: (m, 0)),
                pl.BlockSpec((None, k, tn), lambda j, m: (layer, 0, j)),
                pl.BlockSpec((None, k, tn), lambda j, m: (layer, 0, hoff + j))]
    args = [x, w, w]
    scratch = [pltpu.VMEM((k, tn), BF16), pltpu.VMEM((k, tn), BF16)]
    if mode == "glu":
        b3 = bias.reshape(bias.shape[0], 1, n2)
        in_specs += [pl.BlockSpec((None, 1, tn), lambda j, m: (layer, 0, j)),
                     pl.BlockSpec((None, 1, tn), lambda j, m: (layer, 0, hoff + j))]
        args += [b3, b3]
        out_shape = jax.ShapeDtypeStruct((mrows, half), F32)
        out_specs = pl.BlockSpec((tm, tn), lambda j, m: (m, j))
    else:
        taps = dw.shape[1]
        in_specs += [pl.BlockSpec((None, taps, tn), lambda j, m: (layer, 0, j)),
                     pl.BlockSpec((None, taps, tn), lambda j, m: (layer, 0, hoff + j))]
        args += [dw, dw]
        c0, c1 = ctx
        lo = pl.BlockSpec((tm, tn), lambda j, m: (m, j))
        hi = pl.BlockSpec((tm, tn), lambda j, m: (m, hoff + j))
        in_specs += [lo, hi, lo, hi]
        args += [c0, c0, c1, c1]
        out_shape = (jax.ShapeDtypeStruct((mrows, half), BF16),) + (jax.ShapeDtypeStruct((mrows, half), F32),) * 2
        out_specs = (lo, lo, lo)
    return pl.pallas_call(
        functools.partial(_pair_kernel, mode=mode, rs=rs),
        grid=(half // tn, mrows // tm),
        in_specs=in_specs,
        out_specs=out_specs,
        out_shape=out_shape,
        scratch_shapes=scratch,
        compiler_params=_params("arbitrary", "arbitrary"),
        name="pair_" + mode,
    )(*args)


_FFN_CHUNK_ROWS = 128


def _ffn_up_kernel(x_ref, wa_ref, wb_ref, dwa_ref, dwb_ref, o_ref, ta_ref, tb_ref,
                   wabf_ref, wbbf_ref, ua_ref, ub_ref, *, tm, rs):
    m = pl.program_id(1)
    rc = min(_FFN_CHUNK_ROWS, tm)

    @pl.when(m == 0)
    def _():
        wabf_ref[...] = wa_ref[...].astype(BF16)
        wbbf_ref[...] = wb_ref[...].astype(BF16)

    @pl.when((m * tm) % rs == 0)
    def _():
        ua_ref[0:8, :] = jnp.zeros((8, ua_ref.shape[1]), F32)
        ub_ref[0:8, :] = jnp.zeros((8, ub_ref.shape[1]), F32)

    x = x_ref[...]
    ua_ref[8:tm + 8, :] = jnp.dot(x, wabf_ref[...], preferred_element_type=F32)
    ub_ref[8:tm + 8, :] = jnp.dot(x, wbbf_ref[...], preferred_element_type=F32)

    def chunk(c, carry):
        r0 = pl.multiple_of(c * rc, rc)

        def taps(u_ref, dw_ref):
            u = u_ref[pl.ds(r0, rc + 8), :]
            y = dw_ref[0:1, :] * pltpu.roll(u, 2, 0) + dw_ref[1:2, :] * pltpu.roll(u, 1, 0) + dw_ref[2:3, :] * u
            return y[8:, :]

        gate = taps(ua_ref, dwa_ref)
        val = taps(ub_ref, dwb_ref)
        o_ref[pl.ds(r0, rc), :] = ((gate * jax.nn.sigmoid(gate)) * val).astype(o_ref.dtype)
        return carry

    lax.fori_loop(0, tm // rc, chunk, 0)

    @pl.when(((m + 1) * tm) % rs == 0)
    def _():
        ta_ref[...] = ua_ref[tm:tm + 8, :]
        tb_ref[...] = ub_ref[tm:tm + 8, :]

    ua_ref[0:8, :] = ua_ref[tm:tm + 8, :]
    ub_ref[0:8, :] = ub_ref[tm:tm + 8, :]


def _ffn_up(x, w, layer, dw, *, rs, tm_pref=1024, tn_pref=256):
    mrows, k = x.shape
    half = w.shape[2] // 2
    taps = dw.shape[1]
    tm = _row_tile(mrows, tm_pref)
    tn = _col_tile(half, tn_pref)
    hoff = half // tn
    assert rs % tm == 0 and taps == 3
    tail_shape = jax.ShapeDtypeStruct((mrows // rs, 8, half), F32)
    tail_spec = pl.BlockSpec((None, 8, tn), lambda j, m: ((m * tm) // rs, 0, j))
    return pl.pallas_call(
        functools.partial(_ffn_up_kernel, tm=tm, rs=rs),
        grid=(half // tn, mrows // tm),
        in_specs=[pl.BlockSpec((tm, k), lambda j, m: (m, 0)),
                  pl.BlockSpec((None, k, tn), lambda j, m: (layer, 0, j)),
                  pl.BlockSpec((None, k, tn), lambda j, m: (layer, 0, hoff + j)),
                  pl.BlockSpec((None, taps, tn), lambda j, m: (layer, 0, j)),
                  pl.BlockSpec((None, taps, tn), lambda j, m: (layer, 0, hoff + j))],
        out_specs=(pl.BlockSpec((tm, tn), lambda j, m: (m, j)), tail_spec, tail_spec),
        out_shape=(jax.ShapeDtypeStruct((mrows, half), BF16), tail_shape, tail_shape),
        scratch_shapes=[pltpu.VMEM((k, tn), BF16)] * 2 + [pltpu.VMEM((tm + 8, tn), F32)] * 2,
        compiler_params=_params("arbitrary", "arbitrary"),
        name="ffn_up",
    )(x, w, w, dw, dw)


def _band_bias(n_keys, n_steps):
    qi = lax.broadcasted_iota(jnp.int32, (Q_BLOCK, n_keys), 0)
    ki = lax.broadcasted_iota(jnp.int32, (Q_BLOCK, n_keys), 1)
    dist = qi + (n_keys - Q_BLOCK) - ki
    return jnp.where((dist >= 0) & (dist <= n_steps), 0.0, -jnp.inf).astype(F32)


def _attn_prompt_kernel(*refs, seq):
    qkv_refs, out_ref, scr = refs[:3 * N_GROUPS], refs[3 * N_GROUPS], refs[3 * N_GROUPS + 1:]
    scale = HEAD_DIM ** -0.5
    nt = (((1,), (1,)), ((), ()))
    for g, (window, dil) in enumerate(DILATED_GROUPS):
        q_ref, k_ref, v_ref = qkv_refs[3 * g:3 * g + 3]
        o_scr, l_scr = scr[2 * g], scr[2 * g + 1]
        n_steps = window // dil
        bias_first = _band_bias(Q_BLOCK, n_steps)
        bias_full = _band_bias(2 * Q_BLOCK, n_steps)
        for r in range(dil):
            for blk in range(seq // dil // Q_BLOCK):
                start = r + dil * Q_BLOCK * blk

                def rows(first, count):
                    return pl.ds(first, count, stride=dil) if dil > 1 else pl.ds(first, count)

                q = q_ref[rows(start, Q_BLOCK), :].astype(BF16)
                if blk == 0:
                    keys, bias = rows(start, Q_BLOCK), bias_first
                else:
                    keys, bias = rows(start - dil * Q_BLOCK, 2 * Q_BLOCK), bias_full
                k = k_ref[keys, :].astype(BF16)
                v = v_ref[keys, :].astype(BF16)
                s = lax.dot_general(q, k, nt, preferred_element_type=F32) * scale + bias
                mx = jnp.max(s, axis=-1, keepdims=True)
                p = jnp.exp(s - mx)
                den = jnp.sum(p, axis=-1, keepdims=True)
                o_scr[rows(start, Q_BLOCK), :] = jnp.dot(p.astype(BF16), v, preferred_element_type=F32) / den
                l_scr[rows(start, Q_BLOCK), :] = jnp.broadcast_to(mx + jnp.log(den), (Q_BLOCK, HEAD_DIM))
    l1, l2, l3 = scr[1][...], scr[3][...], scr[5][...]
    mx = jnp.maximum(jnp.maximum(l1, l2), l3)
    e1, e2, e3 = jnp.exp(l1 - mx), jnp.exp(l2 - mx), jnp.exp(l3 - mx)
    num = e1 * scr[0][...] + e2 * scr[2][...] + e3 * scr[4][...]
    out_ref[...] = (num / (e1 + e2 + e3)).astype(out_ref.dtype)


def _attn_prompt(qkv_t, nseq, seq):
    heads = qkv_t.shape[0] // (3 * N_GROUPS)
    for window, dil in DILATED_GROUPS:
        assert seq % (dil * Q_BLOCK) == 0 and window // dil <= Q_BLOCK
    in_specs = [pl.BlockSpec((None, seq, HEAD_DIM), lambda b, h, c=c: (c * heads + h, b, 0))
                for c in range(3 * N_GROUPS)]
    return pl.pallas_call(
        functools.partial(_attn_prompt_kernel, seq=seq),
        grid=(nseq, heads),
        in_specs=in_specs,
        out_specs=pl.BlockSpec((seq, HEAD_DIM), lambda b, h: (b, h)),
        out_shape=jax.ShapeDtypeStruct((nseq * seq, heads * HEAD_DIM), BF16),
        scratch_shapes=[pltpu.VMEM((seq, HEAD_DIM), F32)] * (2 * N_GROUPS),
        compiler_params=_params("arbitrary", "arbitrary"),
        name="attn_prompt",
    )(*([qkv_t] * (3 * N_GROUPS)))


def _attn_sample_kernel(*refs, heads, steps):
    qkv_refs, cache_refs, o_ref = refs[:3 * N_GROUPS], refs[3 * N_GROUPS:4 * N_GROUPS], refs[4 * N_GROUPS]
    assert heads & (heads - 1) == 0
    t = pl.program_id(1)
    scale = HEAD_DIM ** -0.5
    nt = (((1,), (1,)), ((), ()))
    log_heads = jnp.int32(heads.bit_length() - 1)

    def columns(n_pos):
        col = lax.broadcasted_iota(jnp.int32, (heads, n_pos * heads), 1)
        row = lax.broadcasted_iota(jnp.int32, (heads, n_pos * heads), 0)
        return (col & (heads - 1)) == row, lax.shift_right_logical(col, log_heads)

    own_c, pos_c = columns(Q_BLOCK)
    own_n, pos_n = columns(steps)
    scores, values = [], []
    for g, (_, dil) in enumerate(DILATED_GROUPS):
        q_ref, kn_ref, vn_ref = qkv_refs[3 * g:3 * g + 3]
        c_ref = cache_refs[g]
        q = q_ref[...].astype(BF16)
        kc = c_ref[:, 0:heads, :].reshape(Q_BLOCK * heads, HEAD_DIM).astype(BF16)
        kn = kn_ref[...].reshape(steps * heads, HEAD_DIM).astype(BF16)
        sc = lax.dot_general(q, kc, nt, preferred_element_type=F32) * scale
        sn = lax.dot_general(q, kn, nt, preferred_element_type=F32) * scale
        sc = jnp.where(own_c & (pos_c >= t // dil), sc, -jnp.inf)
        back = t - pos_n
        sn = jnp.where(own_n & (back >= 0) & ((back & (dil - 1)) == 0), sn, -jnp.inf)
        scores += [sc, sn]
        values += [c_ref[:, heads:2 * heads, :].reshape(Q_BLOCK * heads, HEAD_DIM).astype(BF16),
                   vn_ref[...].reshape(steps * heads, HEAD_DIM).astype(BF16)]
    mx = functools.reduce(jnp.maximum, [jnp.max(s, axis=-1, keepdims=True) for s in scores])
    den = jnp.zeros((heads, 1), F32)
    acc = jnp.zeros((heads, HEAD_DIM), F32)
    for s, v in zip(scores, values):
        p = jnp.exp(s - mx)
        den = den + jnp.sum(p, axis=-1, keepdims=True)
        acc = acc + jnp.dot(p.astype(BF16), v, preferred_element_type=F32)
    o_ref[...] = acc / den


def _attn_sample(qkv, caches, layer, nseq, steps):
    aw = qkv.shape[1] // (3 * N_GROUPS)
    heads = aw // HEAD_DIM
    qkv4 = qkv.reshape(nseq, steps, 3 * N_GROUPS * heads, HEAD_DIM)
    in_specs, args = [], []
    for g in range(N_GROUPS):
        in_specs += [pl.BlockSpec((None, None, heads, HEAD_DIM), lambda b, t, g=g: (b, t, 3 * g, 0)),
                     pl.BlockSpec((None, steps, heads, HEAD_DIM), lambda b, t, g=g: (b, 0, 3 * g + 1, 0)),
                     pl.BlockSpec((None, steps, heads, HEAD_DIM), lambda b, t, g=g: (b, 0, 3 * g + 2, 0))]
        args += [qkv4] * 3
    for (window, dil), cache in zip(DILATED_GROUPS, caches):
        assert cache.shape[2] == window and window // dil == Q_BLOCK
        args.append(cache.reshape(cache.shape[0] * nseq, Q_BLOCK, dil, 2 * heads, HEAD_DIM))
        in_specs.append(pl.BlockSpec((None, Q_BLOCK, None, 2 * heads, HEAD_DIM),
                                     lambda b, t, dil=dil: (layer * nseq + b, 0, t % dil, 0, 0)))
    out = pl.pallas_call(
        functools.partial(_attn_sample_kernel, heads=heads, steps=steps),
        grid=(nseq, steps),
        in_specs=in_specs,
        out_specs=pl.BlockSpec((None, None, heads, HEAD_DIM), lambda b, t: (b, t, 0, 0)),
        out_shape=jax.ShapeDtypeStruct((nseq, steps, heads, HEAD_DIM), F32),
        compiler_params=_params("arbitrary", "arbitrary"),
        name="attn_sample",
    )(*args)
    return out.reshape(nseq * steps, aw)


_SLIDE_BLOCK_BYTES = 8 * 1024 * 1024


def _slide_kernel(main_ref, next_ref, new_ref, o_ref, *, groups):
    i = pl.program_id(1)
    last = pl.num_programs(1) - 1
    if groups > 1:
        o_ref[0:groups - 1] = main_ref[1:groups]

    @pl.when(i < last)
    def _():
        o_ref[groups - 1] = next_ref[0]

    @pl.when(i == last)
    def _():
        o_ref[groups - 1] = new_ref[...]


def _slide_cache(cache, layer, new_rows):
    layers, nseq, window = cache.shape[:3]
    steps = new_rows.shape[1]
    rows = steps * cache.shape[3] * cache.shape[4]
    assert window % steps == 0 and rows % 8 == 0
    ngroups = window // steps
    groups = _row_tile(ngroups, max(1, _SLIDE_BLOCK_BYTES // (rows * HEAD_DIM * 4)))
    view = cache.reshape(layers * nseq, ngroups, rows, HEAD_DIM)
    blk = (None, groups, rows, HEAD_DIM)
    out = pl.pallas_call(
        functools.partial(_slide_kernel, groups=groups),
        grid=(nseq, ngroups // groups),
        in_specs=[pl.BlockSpec(blk, lambda b, i: (layer * nseq + b, i, 0, 0)),
                  pl.BlockSpec((None, 1, rows, HEAD_DIM),
                               lambda b, i: (layer * nseq + b, jnp.minimum((i + 1) * groups, ngroups - 1), 0, 0)),
                  pl.BlockSpec((None, rows, HEAD_DIM), lambda b, i: (b, 0, 0))],
        out_specs=pl.BlockSpec(blk, lambda b, i: (b, i, 0, 0)),
        out_shape=jax.ShapeDtypeStruct((nseq, ngroups, rows, HEAD_DIM), F32),
        compiler_params=_params("arbitrary", "arbitrary"),
        name="cache_slide",
    )(view, view, new_rows.reshape(nseq, rows, HEAD_DIM))
    return out.reshape((nseq,) + cache.shape[2:])


_CONV_ROWS = 32
_CONV_COLS = 512


def _dwln_kernel(u_ref, ctx_ref, w_ref, bdw_ref, g_ref, b_ref, o_ref, ext_ref, y_ref, *, tm, rs, taps, zero_start):
    m = pl.program_id(0)
    if zero_start:
        at_start = (m * tm) % rs == 0

        @pl.when(at_start)
        def _():
            ext_ref[0:CONV_CTX_ROWS, :] = jnp.zeros((CONV_CTX_ROWS, ext_ref.shape[1]), F32)

        @pl.when(jnp.logical_not(at_start))
        def _():
            ext_ref[0:CONV_CTX_ROWS, :] = ctx_ref[...]
    else:
        ext_ref[0:CONV_CTX_ROWS, :] = ctx_ref[...]
    ext_ref[CONV_CTX_ROWS:CONV_CTX_ROWS + tm, :] = u_ref[...]

    d = u_ref.shape[1]
    rc = min(_CONV_ROWS, tm)
    cw = min(_CONV_COLS, d)
    first = CONV_CTX_ROWS - (taps - 1)
    for r0 in range(0, tm, rc):
        for c0 in range(0, d, cw):
            acc = jnp.zeros((rc, cw), F32)
            for k in range(taps):
                acc = acc + w_ref[k:k + 1, c0:c0 + cw] * ext_ref[r0 + first + k:r0 + first + k + rc, c0:c0 + cw]
            y_ref[r0:r0 + rc, c0:c0 + cw] = acc + bdw_ref[:, c0:c0 + cw]

    y = y_ref[...]
    mu = jnp.mean(y, axis=-1, keepdims=True)
    yc = y - mu
    var = jnp.mean(yc * yc, axis=-1, keepdims=True)
    z = yc * lax.rsqrt(var + LN_EPS) * g_ref[...] + b_ref[...]
    o_ref[...] = (z * jax.nn.sigmoid(z)).astype(o_ref.dtype)


def _dwln(u, ctx, layer, w_dw, b_dw, ln_g, ln_b, *, rs, tm):
    mrows, d = u.shape
    taps = w_dw.shape[1]
    assert taps - 1 <= CONV_CTX_ROWS and tm % 8 == 0
    wpad = jnp.pad(w_dw, ((0, 0), (0, CONV_CTX_ROWS - taps), (0, 0)))
    zero_start = ctx is None
    if zero_start:
        assert tm % CONV_CTX_ROWS == 0
        per = tm // CONV_CTX_ROWS
        ctx_arr = u
        ctx_spec = pl.BlockSpec((CONV_CTX_ROWS, d), lambda m: (jnp.maximum(m * per - 1, 0), 0))
    else:
        assert tm == rs
        ctx_arr = ctx
        ctx_spec = pl.BlockSpec((None, CONV_CTX_ROWS, d), lambda m: (m, 0, 0))
    vec = lambda a: a.reshape(a.shape[0], 1, d)
    vec_spec = pl.BlockSpec((None, 1, d), lambda m: (layer, 0, 0))
    return pl.pallas_call(
        functools.partial(_dwln_kernel, tm=tm, rs=rs, taps=taps, zero_start=zero_start),
        grid=(mrows // tm,),
        in_specs=[pl.BlockSpec((tm, d), lambda m: (m, 0)), ctx_spec,
                  pl.BlockSpec((None, CONV_CTX_ROWS, d), lambda m: (layer, 0, 0)),
                  vec_spec, vec_spec, vec_spec],
        out_specs=pl.BlockSpec((tm, d), lambda m: (m, 0)),
        out_shape=jax.ShapeDtypeStruct((mrows, d), BF16),
        scratch_shapes=[pltpu.VMEM((CONV_CTX_ROWS + tm, d), F32), pltpu.VMEM((tm, d), F32)],
        compiler_params=_params("arbitrary"),
        name="dwconv_ln",
    )(u, ctx_arr, wpad, vec(b_dw), vec(ln_g), vec(ln_b))


def _trunk(x, mod, nseq, rs, params, kv_caches, conv_states, ffn_states):
    (w_qkv, w_o, w_pw1, b_pw1, w_dw, b_dw, ln_g, ln_b, w_pw2, b_pw2,
     w_up, w_ffn_dw, w_down, final_norm_g) = params
    depth = w_up.shape[0]
    d = x.shape[1]
    prompt = kv_caches is None
    new_kv = [[] for _ in range(N_GROUPS)]
    new_conv, new_ffn = [], []
    for i in range(depth):
        j = i // 2
        h = _modulate(x, mod, i, 0)
        if i % 2 == 0:
            heads = w_qkv.shape[2] // (3 * N_GROUPS * HEAD_DIM)
            if prompt:
                qkv_t = _mm(h, w_qkv, j, out_dtype=F32, head_major=True)
                mixed = _attn_prompt(qkv_t, nseq, rs)
                qkv5 = qkv_t.reshape(N_GROUPS, 3, heads, nseq, rs, HEAD_DIM)
                for g, (window, _) in enumerate(DILATED_GROUPS):
                    keep = min(window, rs)
                    new_kv[g].append(jnp.transpose(qkv5[g, 1:, :, :, rs - keep:], (2, 3, 0, 1, 4)))
            else:
                qkv = _mm(h, w_qkv, j, out_dtype=F32)
                mixed = _attn_sample(qkv, kv_caches, j, nseq, rs).astype(BF16)
                qkv5 = qkv.reshape(nseq, rs, N_GROUPS, 3, heads, HEAD_DIM)
                for g in range(N_GROUPS):
                    new_kv[g].append(_slide_cache(kv_caches[g], j, qkv5[:, :, g, 1:]))
            x = _mm(mixed, w_o, j, out_dtype=F32, res=x, gate=(mod, i, 2))
        else:
            u = _pair_mm(h, w_pw1, j, mode="glu", rs=rs, bias=b_pw1)
            taps = w_dw.shape[1]
            if prompt:
                y = _dwln(u, None, j, w_dw, b_dw, ln_g, ln_b, rs=rs, tm=128)
                new_conv.append(u.reshape(nseq, rs, d)[:, rs - (taps - 1):])
            else:
                state = conv_states[j]
                ctx = jnp.pad(state, ((0, 0), (CONV_CTX_ROWS - (taps - 1), 0), (0, 0)))
                y = _dwln(u, ctx, j, w_dw, b_dw, ln_g, ln_b, rs=rs, tm=rs)
                u_all = jnp.concatenate([state, u.reshape(nseq, rs, d)], axis=1)
                new_conv.append(u_all[:, -(taps - 1):])
            x = _mm(y, w_pw2, j, out_dtype=F32, bias=b_pw2, res=x, gate=(mod, i, 2))
        h = _modulate(x, mod, i, 3)
        if prompt:
            act, ta, tb = _ffn_up(h, w_up, i, w_ffn_dw, rs=rs)
            new_ffn.append(jnp.concatenate([ta[:, 6:8], tb[:, 6:8]], axis=-1))
        else:
            state = ffn_states[i]
            c0 = jnp.repeat(state[:, 0], rs, axis=0)
            c1 = jnp.repeat(state[:, 1], rs, axis=0)
            act, ta, tb = _pair_mm(h, w_up, i, mode="ffn", rs=rs, dw=w_ffn_dw, ctx=(c0, c1))
            u_ffn = jnp.concatenate([ta, tb], axis=-1).reshape(nseq, rs, -1)
            new_ffn.append(jnp.concatenate([state, u_ffn], axis=1)[:, -state.shape[1]:])
        x = _mm(act, w_down, i, out_dtype=F32, res=x, gate=(mod, i, 5), tm_pref=256, tn_pref=512,
                single_buffer_w=True)
    y = _final_norm(x, final_norm_g)
    return y, [jnp.stack(l) for l in new_kv], jnp.stack(new_conv), jnp.stack(new_ffn)


def kernel(x_prompt, x_sample, c_prompt, c_sample, cache_kv_g1, cache_kv_g2, cache_kv_g3, state_conv, state_ffn_conv, w_ada, b_ada, w_qkv, w_o, w_pw1, b_pw1, w_dw, b_dw, ln_g, ln_b, w_pw2, b_pw2, w_up, w_ffn_dw, w_down, final_norm_g):
    nb, seq, d = x_prompt.shape
    ndb, steps, _ = x_sample.shape
    assert nb + ndb <= ADA_ROWS
    params = (w_qkv, w_o, w_pw1, b_pw1, w_dw, b_dw, ln_g, ln_b, w_pw2, b_pw2,
              w_up, w_ffn_dw, w_down, final_norm_g)

    c_rows = jnp.concatenate([c_prompt, c_sample, jnp.zeros((ADA_ROWS - nb - ndb, d), F32)], axis=0)
    ada = _adaln(c_rows, w_ada, b_ada)
    mod_p = _Mod(ada, False, seq)
    mod_s = _Mod(jnp.repeat(ada[:, nb:nb + ndb], steps, axis=1), True, steps)

    y_p, kv_p, conv_p, ffn_p = _trunk(x_prompt.reshape(nb * seq, d), mod_p, nb, seq, params, None, None, None)
    y_s, kv_s, conv_s, ffn_s = _trunk(x_sample.reshape(ndb * steps, d), mod_s, ndb, steps, params,
                                      (cache_kv_g1, cache_kv_g2, cache_kv_g3), state_conv, state_ffn_conv)
    return (y_p.reshape(nb, seq, d), y_s.reshape(ndb, steps, d), kv_p[0], kv_p[1], kv_p[2], conv_p, ffn_p,
            kv_s[0], kv_s[1], kv_s[2], conv_s, ffn_s)
```

```python
import functools

import jax
import jax.numpy as jnp
from jax import lax
from jax.experimental import pallas as pl
from jax.experimental.pallas import tpu as pltpu

F32 = jnp.float32
BF16 = jnp.bfloat16

HEAD_DIM = 128
Q_BLOCK = 128
DILATED_GROUPS = ((128, 1), (512, 4), (2048, 16))
N_GROUPS = len(DILATED_GROUPS)
RMS_EPS = 1e-6
LN_EPS = 1e-5
ADA_ROWS = 16
CONV_CTX_ROWS = 32
V7X_VMEM_LIMIT_BYTES = 56 * 1024 * 1024


def _params(*sem):
    return pltpu.CompilerParams(dimension_semantics=sem, vmem_limit_bytes=V7X_VMEM_LIMIT_BYTES)


def _col_tile(n, pref):
    if n <= pref:
        return n
    t = pref - pref % 128
    while t >= 128:
        if n % t == 0:
            return t
        t -= 128
    raise ValueError(f"no lane-aligned tile for {n}")


def _row_tile(m, pref):
    t = 1
    while t * 2 <= pref and m % (t * 2) == 0:
        t *= 2
    return t


class _Mod:
    def __init__(self, arr, per_row, rows_per_seq):
        self.arr, self.per_row, self.rs = arr, per_row, rows_per_seq


def _mod_value(ref, m, tm, rs, per_row):
    if per_row:
        return ref[...]
    return ref[pl.ds((m * tm) // rs, 1), :]


def _adaln_kernel(c_ref, w_ref, b_ref, o_ref):
    c = c_ref[...]
    s = (c * jax.nn.sigmoid(c)).astype(BF16)
    o_ref[...] = jnp.dot(s, w_ref[...].astype(BF16), preferred_element_type=F32) + b_ref[...]


def _adaln(c_rows, w_ada, b_ada):
    depth, d, n = w_ada.shape
    tn = _col_tile(n, 512)
    return pl.pallas_call(
        _adaln_kernel,
        grid=(depth, n // tn),
        in_specs=[pl.BlockSpec((ADA_ROWS, d), lambda i, j: (0, 0)),
                  pl.BlockSpec((None, d, tn), lambda i, j: (i, 0, j)),
                  pl.BlockSpec((None, 1, tn), lambda i, j: (i, 0, j))],
        out_specs=pl.BlockSpec((None, ADA_ROWS, tn), lambda i, j: (i, 0, j)),
        out_shape=jax.ShapeDtypeStruct((depth, ADA_ROWS, n), F32),
        compiler_params=_params("arbitrary", "arbitrary"),
        name="adaln",
    )(c_rows, w_ada, b_ada.reshape(depth, 1, n))


def _modulate_kernel(x_ref, sh_ref, sc_ref, o_ref, *, tm, rs, per_row):
    m = pl.program_id(0)
    x = x_ref[...]
    r = lax.rsqrt(jnp.mean(x * x, axis=-1, keepdims=True) + RMS_EPS)
    sh = _mod_value(sh_ref, m, tm, rs, per_row)
    sc = _mod_value(sc_ref, m, tm, rs, per_row)
    o_ref[...] = ((x * r) * (1.0 + sc) + sh).astype(o_ref.dtype)


def _modulate(x, mod, layer, shift_chunk):
    mrows, d = x.shape
    tm = _row_tile(mrows, 256)
    if mod.per_row:
        spec = lambda c: pl.BlockSpec((None, tm, d), lambda m: (layer, m, c))
    else:
        spec = lambda c: pl.BlockSpec((None, ADA_ROWS, d), lambda m: (layer, 0, c))
    return pl.pallas_call(
        functools.partial(_modulate_kernel, tm=tm, rs=mod.rs, per_row=mod.per_row),
        grid=(mrows // tm,),
        in_specs=[pl.BlockSpec((tm, d), lambda m: (m, 0)), spec(shift_chunk), spec(shift_chunk + 1)],
        out_specs=pl.BlockSpec((tm, d), lambda m: (m, 0)),
        out_shape=jax.ShapeDtypeStruct((mrows, d), BF16),
        compiler_params=_params("arbitrary"),
        name="modulate",
    )(x, mod.arr, mod.arr)


def _final_norm_kernel(x_ref, g_ref, o_ref):
    x = x_ref[...]
    r = lax.rsqrt(jnp.mean(x * x, axis=-1, keepdims=True) + RMS_EPS)
    o_ref[...] = (x * r) * g_ref[...]


def _final_norm(x, g):
    mrows, d = x.shape
    tm = _row_tile(mrows, 256)
    return pl.pallas_call(
        _final_norm_kernel,
        grid=(mrows // tm,),
        in_specs=[pl.BlockSpec((tm, d), lambda m: (m, 0)), pl.BlockSpec((1, d), lambda m: (0, 0))],
        out_specs=pl.BlockSpec((tm, d), lambda m: (m, 0)),
        out_shape=jax.ShapeDtypeStruct((mrows, d), F32),
        compiler_params=_params("arbitrary"),
        name="final_norm",
    )(x, g.reshape(1, d))


def _bf16_tile(w_ref, keep_ref, m):
    if w_ref.dtype == BF16:
        return w_ref[...]

    @pl.when(m == 0)
    def _():
        keep_ref[...] = w_ref[...].astype(BF16)

    return keep_ref[...]


def _weight_spec(w, layer, k, tn, index, **mode):
    if w.dtype == BF16:
        return pl.BlockSpec((k, tn), lambda j, m: (0, index(j)), **mode)
    return pl.BlockSpec((None, k, tn), lambda j, m: (layer, 0, index(j)), **mode)


def _mm_kernel(*refs, has_bias, has_res, tm, rs, per_row):
    it = iter(refs)
    x_ref, w_ref = next(it), next(it)
    b_ref = next(it) if has_bias else None
    res_ref, g_ref = (next(it), next(it)) if has_res else (None, None)
    o_ref, keep_ref = next(it), next(it, None)
    m = pl.program_id(1)
    acc = jnp.dot(x_ref[...], _bf16_tile(w_ref, keep_ref, m), preferred_element_type=F32)
    if has_bias:
        acc = acc + b_ref[...]
    if has_res:
        acc = res_ref[...] + _mod_value(g_ref, m, tm, rs, per_row) * acc
    if len(o_ref.shape) == 3:
        for c in range(o_ref.shape[0]):
            o_ref[c] = acc[:, c * HEAD_DIM:(c + 1) * HEAD_DIM].astype(o_ref.dtype)
    else:
        o_ref[...] = acc.astype(o_ref.dtype)


def _mm(x, w, layer, *, out_dtype, bias=None, res=None, gate=None, tm_pref=1024, tn_pref=512,
        head_major=False, single_buffer_w=False, emit_bf16=False):
    mrows, k = x.shape
    n = w.shape[-1]
    tm = _row_tile(mrows, tm_pref)
    tn = _col_tile(n, tn_pref)
    w_mode = dict(pipeline_mode=pl.Buffered(1)) if single_buffer_w else {}
    in_specs = [pl.BlockSpec((tm, k), lambda j, m: (m, 0)),
                _weight_spec(w, layer, k, tn, lambda j: j, **w_mode)]
    args = [x, w]
    if bias is not None:
        in_specs.append(pl.BlockSpec((None, 1, tn), lambda j, m: (layer, 0, j)))
        args.append(bias.reshape(bias.shape[0], 1, n))
    rs, per_row = 1, True
    if res is not None:
        mod, gl, gate_chunk = gate
        rs, per_row = mod.rs, mod.per_row
        goff = gate_chunk * (n // tn)
        in_specs.append(pl.BlockSpec((tm, tn), lambda j, m: (m, j)))
        if per_row:
            in_specs.append(pl.BlockSpec((None, tm, tn), lambda j, m: (gl, m, goff + j)))
        else:
            in_specs.append(pl.BlockSpec((None, ADA_ROWS, tn), lambda j, m: (gl, 0, goff + j)))
        args += [res, mod.arr]
    if head_major:
        out_spec = pl.BlockSpec((tn // HEAD_DIM, tm, HEAD_DIM), lambda j, m: (j, m, 0))
        out_shape = jax.ShapeDtypeStruct((n // HEAD_DIM, mrows, HEAD_DIM), out_dtype)
    else:
        out_spec = pl.BlockSpec((tm, tn), lambda j, m: (m, j))
        out_shape = jax.ShapeDtypeStruct((mrows, n), out_dtype)
    scratch = []
    if emit_bf16:
        out_spec = (out_spec, pl.BlockSpec((k, tn), lambda j, m: (0, j)))
        out_shape = (out_shape, jax.ShapeDtypeStruct((k, n), BF16))
    elif w.dtype != BF16:
        scratch = [pltpu.VMEM((k, tn), BF16)]
    return pl.pallas_call(
        functools.partial(_mm_kernel, has_bias=bias is not None, has_res=res is not None,
                          tm=tm, rs=rs, per_row=per_row),
        grid=(n // tn, mrows // tm),
        in_specs=in_specs,
        out_specs=out_spec,
        out_shape=out_shape,
        scratch_shapes=scratch,
        compiler_params=_params("arbitrary", "arbitrary"),
        name="matmul",
    )(*args)


def _causal_taps(u, dw_ref, row, p0, p1):
    u1 = jnp.where(row == 0, p1, pltpu.roll(u, 1, 0))
    u2 = jnp.where(row == 0, p0, jnp.where(row == 1, p1, pltpu.roll(u, 2, 0)))
    return dw_ref[0:1, :] * u2 + dw_ref[1:2, :] * u1 + dw_ref[2:3, :] * u


def _pair_kernel(*refs, mode, rs):
    it = iter(refs)
    x_ref, wa_ref, wb_ref = next(it), next(it), next(it)
    if mode == "glu":
        ba_ref, bb_ref = next(it), next(it)
    else:
        dwa_ref, dwb_ref = next(it), next(it)
        c0a_ref, c0b_ref, c1a_ref, c1b_ref = next(it), next(it), next(it), next(it)
    o_ref = next(it)
    if mode == "ffn":
        ta_ref, tb_ref = next(it), next(it)
    keep_a_ref, keep_b_ref = next(it, None), next(it, None)
    m = pl.program_id(1)
    x = x_ref[...]
    ua = jnp.dot(x, _bf16_tile(wa_ref, keep_a_ref, m), preferred_element_type=F32)
    ub = jnp.dot(x, _bf16_tile(wb_ref, keep_b_ref, m), preferred_element_type=F32)

    if mode == "glu":
        o_ref[...] = ((ua + ba_ref[...]) * jax.nn.sigmoid(ub + bb_ref[...])).astype(o_ref.dtype)
        return

    assert rs & (rs - 1) == 0
    row = lax.broadcasted_iota(jnp.int32, ua.shape, 0) & (rs - 1)
    gate = _causal_taps(ua, dwa_ref, row, c0a_ref[...], c1a_ref[...])
    val = _causal_taps(ub, dwb_ref, row, c0b_ref[...], c1b_ref[...])
    ta_ref[...] = ua
    tb_ref[...] = ub
    o_ref[...] = ((gate * jax.nn.sigmoid(gate)) * val).astype(o_ref.dtype)


def _pair_weights(w, layer, k, half, tn):
    if isinstance(w, tuple):
        return [_weight_spec(w[0], layer, k, tn, lambda j: j), _weight_spec(w[1], layer, k, tn, lambda j: j)], list(w)
    hoff = half // tn
    return [_weight_spec(w, layer, k, tn, lambda j: j), _weight_spec(w, layer, k, tn, lambda j: hoff + j)], [w, w]


def _pair_mm(x, w, layer, *, mode, rs, bias=None, dw=None, ctx=None, tm_pref=1024, tn_pref=256, emit_bf16=False):
    mrows, k = x.shape
    rounded = isinstance(w, tuple)
    half = w[0].shape[1] if rounded else w.shape[2] // 2
    n2 = 2 * half
    tm = _row_tile(mrows, tm_pref)
    tn = _col_tile(half, tn_pref)
    hoff = half // tn
    assert mode == "glu" or tm % rs == 0
    w_specs, w_args = _pair_weights(w, layer, k, half, tn)
    in_specs = [pl.BlockSpec((tm, k), lambda j, m: (m, 0))] + w_specs
    args = [x] + w_args
    scratch = [] if rounded or emit_bf16 else [pltpu.VMEM((k, tn), BF16), pltpu.VMEM((k, tn), BF16)]
    if mode == "glu":
        b3 = bias.reshape(bias.shape[0], 1, n2)
        in_specs += [pl.BlockSpec((None, 1, tn), lambda j, m: (layer, 0, j)),
                     pl.BlockSpec((None, 1, tn), lambda j, m: (layer, 0, hoff + j))]
        args += [b3, b3]
        out_shape = jax.ShapeDtypeStruct((mrows, half), F32)
        out_specs = pl.BlockSpec((tm, tn), lambda j, m: (m, j))
        if emit_bf16:
            out_shape = (out_shape,) + (jax.ShapeDtypeStruct((k, half), BF16),) * 2
            out_specs = (out_specs,) + (pl.BlockSpec((k, tn), lambda j, m: (0, j)),) * 2
    else:
        taps = dw.shape[1]
        in_specs += [pl.BlockSpec((None, taps, tn), lambda j, m: (layer, 0, j)),
                     pl.BlockSpec((None, taps, tn), lambda j, m: (layer, 0, hoff + j))]
        args += [dw, dw]
        c0, c1 = ctx
        lo = pl.BlockSpec((tm, tn), lambda j, m: (m, j))
        hi = pl.BlockSpec((tm, tn), lambda j, m: (m, hoff + j))
        in_specs += [lo, hi, lo, hi]
        args += [c0, c0, c1, c1]
        out_shape = (jax.ShapeDtypeStruct((mrows, half), BF16),) + (jax.ShapeDtypeStruct((mrows, half), F32),) * 2
        out_specs = (lo, lo, lo)
    return pl.pallas_call(
        functools.partial(_pair_kernel, mode=mode, rs=rs),
        grid=(half // tn, mrows // tm),
        in_specs=in_specs,
        out_specs=out_specs,
        out_shape=out_shape,
        scratch_shapes=scratch,
        compiler_params=_params("arbitrary", "arbitrary"),
        name="pair_" + mode,
    )(*args)


_FFN_CHUNK_ROWS = 128


def _ffn_up_kernel(x_ref, wa_ref, wb_ref, dwa_ref, dwb_ref, o_ref, ta_ref, tb_ref,
                   wabf_ref, wbbf_ref, ua_ref, ub_ref, *, tm, rs):
    m = pl.program_id(1)
    rc = min(_FFN_CHUNK_ROWS, tm)

    @pl.when((m * tm) % rs == 0)
    def _():
        ua_ref[0:8, :] = jnp.zeros((8, ua_ref.shape[1]), F32)
        ub_ref[0:8, :] = jnp.zeros((8, ub_ref.shape[1]), F32)

    x = x_ref[...]
    ua_ref[8:tm + 8, :] = jnp.dot(x, _bf16_tile(wa_ref, wabf_ref, m), preferred_element_type=F32)
    ub_ref[8:tm + 8, :] = jnp.dot(x, _bf16_tile(wb_ref, wbbf_ref, m), preferred_element_type=F32)

    def chunk(c, carry):
        r0 = pl.multiple_of(c * rc, rc)

        def taps(u_ref, dw_ref):
            u = u_ref[pl.ds(r0, rc + 8), :]
            y = dw_ref[0:1, :] * pltpu.roll(u, 2, 0) + dw_ref[1:2, :] * pltpu.roll(u, 1, 0) + dw_ref[2:3, :] * u
            return y[8:, :]

        gate = taps(ua_ref, dwa_ref)
        val = taps(ub_ref, dwb_ref)
        o_ref[pl.ds(r0, rc), :] = ((gate * jax.nn.sigmoid(gate)) * val).astype(o_ref.dtype)
        return carry

    lax.fori_loop(0, tm // rc, chunk, 0)

    @pl.when(((m + 1) * tm) % rs == 0)
    def _():
        ta_ref[...] = ua_ref[tm:tm + 8, :]
        tb_ref[...] = ub_ref[tm:tm + 8, :]

    ua_ref[0:8, :] = ua_ref[tm:tm + 8, :]
    ub_ref[0:8, :] = ub_ref[tm:tm + 8, :]


def _ffn_up(x, w, layer, dw, *, rs, tm_pref=1024, tn_pref=256):
    mrows, k = x.shape
    half = w.shape[2] // 2
    taps = dw.shape[1]
    tm = _row_tile(mrows, tm_pref)
    tn = _col_tile(half, tn_pref)
    hoff = half // tn
    assert rs % tm == 0 and taps == 3
    tail_shape = jax.ShapeDtypeStruct((mrows // rs, 8, half), F32)
    tail_spec = pl.BlockSpec((None, 8, tn), lambda j, m: ((m * tm) // rs, 0, j))
    wbf_shape = jax.ShapeDtypeStruct((k, half), BF16)
    wbf_spec = pl.BlockSpec((k, tn), lambda j, m: (0, j))
    return pl.pallas_call(
        functools.partial(_ffn_up_kernel, tm=tm, rs=rs),
        grid=(half // tn, mrows // tm),
        in_specs=[pl.BlockSpec((tm, k), lambda j, m: (m, 0)),
                  pl.BlockSpec((None, k, tn), lambda j, m: (layer, 0, j)),
                  pl.BlockSpec((None, k, tn), lambda j, m: (layer, 0, hoff + j)),
                  pl.BlockSpec((None, taps, tn), lambda j, m: (layer, 0, j)),
                  pl.BlockSpec((None, taps, tn), lambda j, m: (layer, 0, hoff + j))],
        out_specs=(pl.BlockSpec((tm, tn), lambda j, m: (m, j)), tail_spec, tail_spec, wbf_spec, wbf_spec),
        out_shape=(jax.ShapeDtypeStruct((mrows, half), BF16), tail_shape, tail_shape, wbf_shape, wbf_shape),
        scratch_shapes=[pltpu.VMEM((tm + 8, tn), F32)] * 2,
        compiler_params=_params("arbitrary", "arbitrary"),
        name="ffn_up",
    )(x, w, w, dw, dw)


def _band_bias(n_keys, n_steps):
    qi = lax.broadcasted_iota(jnp.int32, (Q_BLOCK, n_keys), 0)
    ki = lax.broadcasted_iota(jnp.int32, (Q_BLOCK, n_keys), 1)
    dist = qi + (n_keys - Q_BLOCK) - ki
    return jnp.where((dist >= 0) & (dist <= n_steps), 0.0, -jnp.inf).astype(F32)


def _attn_prompt_kernel(*refs, seq):
    qkv_refs, out_ref, scr = refs[:3 * N_GROUPS], refs[3 * N_GROUPS], refs[3 * N_GROUPS + 1:]
    scale = HEAD_DIM ** -0.5
    nt = (((1,), (1,)), ((), ()))
    for g, (window, dil) in enumerate(DILATED_GROUPS):
        q_ref, k_ref, v_ref = qkv_refs[3 * g:3 * g + 3]
        o_scr, l_scr = scr[2 * g], scr[2 * g + 1]
        n_steps = window // dil
        bias_first = _band_bias(Q_BLOCK, n_steps)
        bias_full = _band_bias(2 * Q_BLOCK, n_steps)
        for r in range(dil):
            for blk in range(seq // dil // Q_BLOCK):
                start = r + dil * Q_BLOCK * blk

                def rows(first, count):
                    return pl.ds(first, count, stride=dil) if dil > 1 else pl.ds(first, count)

                q = q_ref[rows(start, Q_BLOCK), :].astype(BF16)
                if blk == 0:
                    keys, bias = rows(start, Q_BLOCK), bias_first
                else:
                    keys, bias = rows(start - dil * Q_BLOCK, 2 * Q_BLOCK), bias_full
                k = k_ref[keys, :].astype(BF16)
                v = v_ref[keys, :].astype(BF16)
                s = lax.dot_general(q, k, nt, preferred_element_type=F32) * scale + bias
                mx = jnp.max(s, axis=-1, keepdims=True)
                p = jnp.exp(s - mx)
                den = jnp.sum(p, axis=-1, keepdims=True)
                o_scr[rows(start, Q_BLOCK), :] = jnp.dot(p.astype(BF16), v, preferred_element_type=F32) / den
                l_scr[rows(start, Q_BLOCK), :] = jnp.broadcast_to(mx + jnp.log(den), (Q_BLOCK, HEAD_DIM))
    l1, l2, l3 = scr[1][...], scr[3][...], scr[5][...]
    mx = jnp.maximum(jnp.maximum(l1, l2), l3)
    e1, e2, e3 = jnp.exp(l1 - mx), jnp.exp(l2 - mx), jnp.exp(l3 - mx)
    num = e1 * scr[0][...] + e2 * scr[2][...] + e3 * scr[4][...]
    out_ref[...] = (num / (e1 + e2 + e3)).astype(out_ref.dtype)


def _attn_prompt(qkv_t, nseq, seq):
    heads = qkv_t.shape[0] // (3 * N_GROUPS)
    for window, dil in DILATED_GROUPS:
        assert seq % (dil * Q_BLOCK) == 0 and window // dil <= Q_BLOCK
    in_specs = [pl.BlockSpec((None, seq, HEAD_DIM), lambda b, h, c=c: (c * heads + h, b, 0))
                for c in range(3 * N_GROUPS)]
    return pl.pallas_call(
        functools.partial(_attn_prompt_kernel, seq=seq),
        grid=(nseq, heads),
        in_specs=in_specs,
        out_specs=pl.BlockSpec((seq, HEAD_DIM), lambda b, h: (b, h)),
        out_shape=jax.ShapeDtypeStruct((nseq * seq, heads * HEAD_DIM), BF16),
        scratch_shapes=[pltpu.VMEM((seq, HEAD_DIM), F32)] * (2 * N_GROUPS),
        compiler_params=_params("arbitrary", "arbitrary"),
        name="attn_prompt",
    )(*([qkv_t] * (3 * N_GROUPS)))


def _attn_sample_kernel(*refs, heads, steps):
    qkv_refs, cache_refs, o_ref = refs[:3 * N_GROUPS], refs[3 * N_GROUPS:4 * N_GROUPS], refs[4 * N_GROUPS]
    assert heads & (heads - 1) == 0
    t = pl.program_id(1)
    scale = HEAD_DIM ** -0.5
    nt = (((1,), (1,)), ((), ()))
    log_heads = jnp.int32(heads.bit_length() - 1)

    def columns(n_pos):
        col = lax.broadcasted_iota(jnp.int32, (heads, n_pos * heads), 1)
        row = lax.broadcasted_iota(jnp.int32, (heads, n_pos * heads), 0)
        return (col & (heads - 1)) == row, lax.shift_right_logical(col, log_heads)

    own_c, pos_c = columns(Q_BLOCK)
    own_n, pos_n = columns(steps)
    scores, values = [], []
    for g, (_, dil) in enumerate(DILATED_GROUPS):
        q_ref, kn_ref, vn_ref = qkv_refs[3 * g:3 * g + 3]
        c_ref = cache_refs[g]
        q = q_ref[...].astype(BF16)
        kc = c_ref[:, 0:heads, :].reshape(Q_BLOCK * heads, HEAD_DIM).astype(BF16)
        kn = kn_ref[...].reshape(steps * heads, HEAD_DIM).astype(BF16)
        sc = lax.dot_general(q, kc, nt, preferred_element_type=F32) * scale
        sn = lax.dot_general(q, kn, nt, preferred_element_type=F32) * scale
        sc = jnp.where(own_c & (pos_c >= t // dil), sc, -jnp.inf)
        back = t - pos_n
        sn = jnp.where(own_n & (back >= 0) & ((back & (dil - 1)) == 0), sn, -jnp.inf)
        scores += [sc, sn]
        values += [c_ref[:, heads:2 * heads, :].reshape(Q_BLOCK * heads, HEAD_DIM).astype(BF16),
                   vn_ref[...].reshape(steps * heads, HEAD_DIM).astype(BF16)]
    mx = functools.reduce(jnp.maximum, [jnp.max(s, axis=-1, keepdims=True) for s in scores])
    den = jnp.zeros((heads, 1), F32)
    acc = jnp.zeros((heads, HEAD_DIM), F32)
    for s, v in zip(scores, values):
        p = jnp.exp(s - mx)
        den = den + jnp.sum(p, axis=-1, keepdims=True)
        acc = acc + jnp.dot(p.astype(BF16), v, preferred_element_type=F32)
    o_ref[...] = acc / den


def _attn_sample(qkv, caches, layer, nseq, steps):
    aw = qkv.shape[1] // (3 * N_GROUPS)
    heads = aw // HEAD_DIM
    qkv4 = qkv.reshape(nseq, steps, 3 * N_GROUPS * heads, HEAD_DIM)
    in_specs, args = [], []
    for g in range(N_GROUPS):
        in_specs += [pl.BlockSpec((None, None, heads, HEAD_DIM), lambda b, t, g=g: (b, t, 3 * g, 0)),
                     pl.BlockSpec((None, steps, heads, HEAD_DIM), lambda b, t, g=g: (b, 0, 3 * g + 1, 0)),
                     pl.BlockSpec((None, steps, heads, HEAD_DIM), lambda b, t, g=g: (b, 0, 3 * g + 2, 0))]
        args += [qkv4] * 3
    for (window, dil), cache in zip(DILATED_GROUPS, caches):
        assert cache.shape[2] == window and window // dil == Q_BLOCK
        args.append(cache.reshape(cache.shape[0] * nseq, Q_BLOCK, dil, 2 * heads, HEAD_DIM))
        in_specs.append(pl.BlockSpec((None, Q_BLOCK, None, 2 * heads, HEAD_DIM),
                                     lambda b, t, dil=dil: (layer * nseq + b, 0, t % dil, 0, 0)))
    out = pl.pallas_call(
        functools.partial(_attn_sample_kernel, heads=heads, steps=steps),
        grid=(nseq, steps),
        in_specs=in_specs,
        out_specs=pl.BlockSpec((None, None, heads, HEAD_DIM), lambda b, t: (b, t, 0, 0)),
        out_shape=jax.ShapeDtypeStruct((nseq, steps, heads, HEAD_DIM), F32),
        compiler_params=_params("arbitrary", "arbitrary"),
        name="attn_sample",
    )(*args)
    return out.reshape(nseq * steps, aw)


_SLIDE_BLOCK_BYTES = 8 * 1024 * 1024


def _slide_kernel(main_ref, next_ref, new_ref, o_ref, *, groups):
    i = pl.program_id(1)
    last = pl.num_programs(1) - 1
    if groups > 1:
        o_ref[0:groups - 1] = main_ref[1:groups]

    @pl.when(i < last)
    def _():
        o_ref[groups - 1] = next_ref[0]

    @pl.when(i == last)
    def _():
        o_ref[groups - 1] = new_ref[...]


def _slide_cache(cache, layer, new_rows):
    layers, nseq, window = cache.shape[:3]
    steps = new_rows.shape[1]
    rows = steps * cache.shape[3] * cache.shape[4]
    assert window % steps == 0 and rows % 8 == 0
    ngroups = window // steps
    groups = _row_tile(ngroups, max(1, _SLIDE_BLOCK_BYTES // (rows * HEAD_DIM * 4)))
    view = cache.reshape(layers * nseq, ngroups, rows, HEAD_DIM)
    blk = (None, groups, rows, HEAD_DIM)
    out = pl.pallas_call(
        functools.partial(_slide_kernel, groups=groups),
        grid=(nseq, ngroups // groups),
        in_specs=[pl.BlockSpec(blk, lambda b, i: (layer * nseq + b, i, 0, 0)),
                  pl.BlockSpec((None, 1, rows, HEAD_DIM),
                               lambda b, i: (layer * nseq + b, jnp.minimum((i + 1) * groups, ngroups - 1), 0, 0)),
                  pl.BlockSpec((None, rows, HEAD_DIM), lambda b, i: (b, 0, 0))],
        out_specs=pl.BlockSpec(blk, lambda b, i: (b, i, 0, 0)),
        out_shape=jax.ShapeDtypeStruct((nseq, ngroups, rows, HEAD_DIM), F32),
        compiler_params=_params("arbitrary", "arbitrary"),
        name="cache_slide",
    )(view, view, new_rows.reshape(nseq, rows, HEAD_DIM))
    return out.reshape((nseq,) + cache.shape[2:])


_CONV_ROWS = 64
_CONV_COLS = 256


def _dwln_kernel(u_ref, ctx_ref, w_ref, bdw_ref, g_ref, b_ref, o_ref, ext_ref, y_ref, *, tm, rs, taps, zero_start):
    m = pl.program_id(0)
    if zero_start:
        at_start = (m * tm) % rs == 0

        @pl.when(at_start)
        def _():
            ext_ref[0:CONV_CTX_ROWS, :] = jnp.zeros((CONV_CTX_ROWS, ext_ref.shape[1]), F32)

        @pl.when(jnp.logical_not(at_start))
        def _():
            ext_ref[0:CONV_CTX_ROWS, :] = ctx_ref[...]
    else:
        ext_ref[0:CONV_CTX_ROWS, :] = ctx_ref[...]
    ext_ref[CONV_CTX_ROWS:CONV_CTX_ROWS + tm, :] = u_ref[...]
    ext_ref[CONV_CTX_ROWS + tm:CONV_CTX_ROWS + tm + 8, :] = jnp.zeros((8, ext_ref.shape[1]), F32)

    d = u_ref.shape[1]
    rc = min(_CONV_ROWS, tm)
    cw = min(_CONV_COLS, d)
    first = CONV_CTX_ROWS - (taps - 1)
    for r0 in range(0, tm, rc):
        for c0 in range(0, d, cw):
            acc = jnp.zeros((rc, cw), F32)
            for s in range(8):
                part = None
                for k in range(taps):
                    if (first + k) % 8 != s:
                        continue
                    a = r0 + first + k - s
                    term = w_ref[k:k + 1, c0:c0 + cw] * ext_ref[a:a + rc + 8, c0:c0 + cw]
                    part = term if part is None else part + term
                if part is not None:
                    acc = acc + part[s:s + rc, :]
            y_ref[r0:r0 + rc, c0:c0 + cw] = acc + bdw_ref[:, c0:c0 + cw]

    y = y_ref[...]
    mu = jnp.mean(y, axis=-1, keepdims=True)
    yc = y - mu
    var = jnp.mean(yc * yc, axis=-1, keepdims=True)
    z = yc * lax.rsqrt(var + LN_EPS) * g_ref[...] + b_ref[...]
    o_ref[...] = (z * jax.nn.sigmoid(z)).astype(o_ref.dtype)


def _dwln(u, ctx, layer, w_dw, b_dw, ln_g, ln_b, *, rs, tm):
    mrows, d = u.shape
    taps = w_dw.shape[1]
    assert taps - 1 <= CONV_CTX_ROWS and tm % 8 == 0
    wpad = jnp.pad(w_dw, ((0, 0), (0, CONV_CTX_ROWS - taps), (0, 0)))
    zero_start = ctx is None
    if zero_start:
        assert tm % CONV_CTX_ROWS == 0
        per = tm // CONV_CTX_ROWS
        ctx_arr = u
        ctx_spec = pl.BlockSpec((CONV_CTX_ROWS, d), lambda m: (jnp.maximum(m * per - 1, 0), 0))
    else:
        assert tm == rs
        ctx_arr = ctx
        ctx_spec = pl.BlockSpec((None, CONV_CTX_ROWS, d), lambda m: (m, 0, 0))
    vec = lambda a: a.reshape(a.shape[0], 1, d)
    vec_spec = pl.BlockSpec((None, 1, d), lambda m: (layer, 0, 0))
    return pl.pallas_call(
        functools.partial(_dwln_kernel, tm=tm, rs=rs, taps=taps, zero_start=zero_start),
        grid=(mrows // tm,),
        in_specs=[pl.BlockSpec((tm, d), lambda m: (m, 0)), ctx_spec,
                  pl.BlockSpec((None, CONV_CTX_ROWS, d), lambda m: (layer, 0, 0)),
                  vec_spec, vec_spec, vec_spec],
        out_specs=pl.BlockSpec((tm, d), lambda m: (m, 0)),
        out_shape=jax.ShapeDtypeStruct((mrows, d), BF16),
        scratch_shapes=[pltpu.VMEM((CONV_CTX_ROWS + tm + 8, d), F32), pltpu.VMEM((tm, d), F32)],
        compiler_params=_params("arbitrary"),
        name="dwconv_ln",
    )(u, ctx_arr, wpad, vec(b_dw), vec(ln_g), vec(ln_b))


def _trunk(x, mod, nseq, rs, params, kv_caches, conv_states, ffn_states, rounded):
    (w_qkv, w_o, w_pw1, b_pw1, w_dw, b_dw, ln_g, ln_b, w_pw2, b_pw2,
     w_up, w_ffn_dw, w_down, final_norm_g) = params
    depth = w_up.shape[0]
    d = x.shape[1]
    prompt = kv_caches is None
    new_kv = [[] for _ in range(N_GROUPS)]
    new_conv, new_ffn = [], []

    def gated(name, xin, w, layer, **kw):
        if prompt:
            out, rounded[name, layer] = _mm(xin, w, layer, out_dtype=F32, emit_bf16=True, **kw)
            return out
        return _mm(xin, rounded[name, layer], layer, out_dtype=F32, **kw)

    for i in range(depth):
        j = i // 2
        h = _modulate(x, mod, i, 0)
        if i % 2 == 0:
            heads = w_qkv.shape[2] // (3 * N_GROUPS * HEAD_DIM)
            if prompt:
                qkv_t, rounded["qkv", j] = _mm(h, w_qkv, j, out_dtype=F32, head_major=True, emit_bf16=True)
                mixed = _attn_prompt(qkv_t, nseq, rs)
                qkv5 = qkv_t.reshape(N_GROUPS, 3, heads, nseq, rs, HEAD_DIM)
                for g, (window, _) in enumerate(DILATED_GROUPS):
                    keep = min(window, rs)
                    new_kv[g].append(jnp.transpose(qkv5[g, 1:, :, :, rs - keep:], (2, 3, 0, 1, 4)))
            else:
                qkv = _mm(h, rounded["qkv", j], j, out_dtype=F32)
                mixed = _attn_sample(qkv, kv_caches, j, nseq, rs).astype(BF16)
                qkv5 = qkv.reshape(nseq, rs, N_GROUPS, 3, heads, HEAD_DIM)
                for g in range(N_GROUPS):
                    new_kv[g].append(_slide_cache(kv_caches[g], j, qkv5[:, :, g, 1:]))
            x = gated("o", mixed, w_o, j, res=x, gate=(mod, i, 2))
        else:
            if prompt:
                u, *rounded["pw1", j] = _pair_mm(h, w_pw1, j, mode="glu", rs=rs, bias=b_pw1, emit_bf16=True)
            else:
                u = _pair_mm(h, tuple(rounded["pw1", j]), j, mode="glu", rs=rs, bias=b_pw1)
            taps = w_dw.shape[1]
            if prompt:
                y = _dwln(u, None, j, w_dw, b_dw, ln_g, ln_b, rs=rs, tm=128)
                new_conv.append(u.reshape(nseq, rs, d)[:, rs - (taps - 1):])
            else:
                state = conv_states[j]
                ctx = jnp.pad(state, ((0, 0), (CONV_CTX_ROWS - (taps - 1), 0), (0, 0)))
                y = _dwln(u, ctx, j, w_dw, b_dw, ln_g, ln_b, rs=rs, tm=rs)
                u_all = jnp.concatenate([state, u.reshape(nseq, rs, d)], axis=1)
                new_conv.append(u_all[:, -(taps - 1):])
            x = gated("pw2", y, w_pw2, j, bias=b_pw2, res=x, gate=(mod, i, 2))
        h = _modulate(x, mod, i, 3)
        if prompt:
            act, ta, tb, *rounded["up", i] = _ffn_up(h, w_up, i, w_ffn_dw, rs=rs)
            new_ffn.append(jnp.concatenate([ta[:, 6:8], tb[:, 6:8]], axis=-1))
        else:
            state = ffn_states[i]
            c0 = jnp.repeat(state[:, 0], rs, axis=0)
            c1 = jnp.repeat(state[:, 1], rs, axis=0)
            act, ta, tb = _pair_mm(h, tuple(rounded["up", i]), i, mode="ffn", rs=rs, dw=w_ffn_dw, ctx=(c0, c1))
            u_ffn = jnp.concatenate([ta, tb], axis=-1).reshape(nseq, rs, -1)
            new_ffn.append(jnp.concatenate([state, u_ffn], axis=1)[:, -state.shape[1]:])
        x = _mm(act, w_down, i, out_dtype=F32, res=x, gate=(mod, i, 5), tm_pref=256, tn_pref=512,
                single_buffer_w=True)
    y = _final_norm(x, final_norm_g)
    return y, [jnp.stack(l) for l in new_kv], jnp.stack(new_conv), jnp.stack(new_ffn)


def kernel(x_prompt, x_sample, c_prompt, c_sample, cache_kv_g1, cache_kv_g2, cache_kv_g3, state_conv, state_ffn_conv, w_ada, b_ada, w_qkv, w_o, w_pw1, b_pw1, w_dw, b_dw, ln_g, ln_b, w_pw2, b_pw2, w_up, w_ffn_dw, w_down, final_norm_g):
    nb, seq, d = x_prompt.shape
    ndb, steps, _ = x_sample.shape
    assert nb + ndb <= ADA_ROWS
    params = (w_qkv, w_o, w_pw1, b_pw1, w_dw, b_dw, ln_g, ln_b, w_pw2, b_pw2,
              w_up, w_ffn_dw, w_down, final_norm_g)

    c_rows = jnp.concatenate([c_prompt, c_sample, jnp.zeros((ADA_ROWS - nb - ndb, d), F32)], axis=0)
    ada = _adaln(c_rows, w_ada, b_ada)
    mod_p = _Mod(ada, False, seq)
    mod_s = _Mod(jnp.repeat(ada[:, nb:nb + ndb], steps, axis=1), True, steps)

    rounded = {}
    y_p, kv_p, conv_p, ffn_p = _trunk(x_prompt.reshape(nb * seq, d), mod_p, nb, seq, params, None, None, None,
                                      rounded)
    y_s, kv_s, conv_s, ffn_s = _trunk(x_sample.reshape(ndb * steps, d), mod_s, ndb, steps, params,
                                      (cache_kv_g1, cache_kv_g2, cache_kv_g3), state_conv, state_ffn_conv, rounded)
    return (y_p.reshape(nb, seq, d), y_s.reshape(ndb, steps, d), kv_p[0], kv_p[1], kv_p[2], conv_p, ffn_p,
            kv_s[0], kv_s[1], kv_s[2], conv_s, ffn_s)
```

```python
import functools

import jax
import jax.numpy as jnp
from jax import lax
from jax.experimental import pallas as pl
from jax.experimental.pallas import tpu as pltpu

F32 = jnp.float32
BF16 = jnp.bfloat16

HEAD_DIM = 128
Q_BLOCK = 128
DILATED_GROUPS = ((128, 1), (512, 4), (2048, 16))
N_GROUPS = len(DILATED_GROUPS)
RMS_EPS = 1e-6
LN_EPS = 1e-5
ADA_ROWS = 16
CONV_CTX_ROWS = 32
V7X_VMEM_LIMIT_BYTES = 56 * 1024 * 1024


def _params(*sem):
    return pltpu.CompilerParams(dimension_semantics=sem, vmem_limit_bytes=V7X_VMEM_LIMIT_BYTES)


def _col_tile(n, pref):
    if n <= pref:
        return n
    t = pref - pref % 128
    while t >= 128:
        if n % t == 0:
            return t
        t -= 128
    raise ValueError(f"no lane-aligned tile for {n}")


def _row_tile(m, pref):
    t = 1
    while t * 2 <= pref and m % (t * 2) == 0:
        t *= 2
    return t


class _Mod:
    def __init__(self, arr, per_row, rows_per_seq):
        self.arr, self.per_row, self.rs = arr, per_row, rows_per_seq


def _mod_value(ref, m, tm, rs, per_row):
    if per_row:
        return ref[...]
    return ref[pl.ds((m * tm) // rs, 1), :]


def _adaln_kernel(c_ref, w_ref, b_ref, o_ref):
    c = c_ref[...]
    s = (c * jax.nn.sigmoid(c)).astype(BF16)
    o_ref[...] = jnp.dot(s, w_ref[...].astype(BF16), preferred_element_type=F32) + b_ref[...]


def _adaln(c_rows, w_ada, b_ada):
    depth, d, n = w_ada.shape
    tn = _col_tile(n, 512)
    return pl.pallas_call(
        _adaln_kernel,
        grid=(depth, n // tn),
        in_specs=[pl.BlockSpec((ADA_ROWS, d), lambda i, j: (0, 0)),
                  pl.BlockSpec((None, d, tn), lambda i, j: (i, 0, j)),
                  pl.BlockSpec((None, 1, tn), lambda i, j: (i, 0, j))],
        out_specs=pl.BlockSpec((None, ADA_ROWS, tn), lambda i, j: (i, 0, j)),
        out_shape=jax.ShapeDtypeStruct((depth, ADA_ROWS, n), F32),
        compiler_params=_params("arbitrary", "arbitrary"),
        name="adaln",
    )(c_rows, w_ada, b_ada.reshape(depth, 1, n))


def _modulate_kernel(x_ref, sh_ref, sc_ref, o_ref, *, tm, rs, per_row):
    m = pl.program_id(0)
    x = x_ref[...]
    r = lax.rsqrt(jnp.mean(x * x, axis=-1, keepdims=True) + RMS_EPS)
    sh = _mod_value(sh_ref, m, tm, rs, per_row)
    sc = _mod_value(sc_ref, m, tm, rs, per_row)
    o_ref[...] = ((x * r) * (1.0 + sc) + sh).astype(o_ref.dtype)


def _modulate(x, mod, layer, shift_chunk):
    mrows, d = x.shape
    tm = _row_tile(mrows, 256)
    if mod.per_row:
        spec = lambda c: pl.BlockSpec((None, tm, d), lambda m: (layer, m, c))
    else:
        spec = lambda c: pl.BlockSpec((None, ADA_ROWS, d), lambda m: (layer, 0, c))
    return pl.pallas_call(
        functools.partial(_modulate_kernel, tm=tm, rs=mod.rs, per_row=mod.per_row),
        grid=(mrows // tm,),
        in_specs=[pl.BlockSpec((tm, d), lambda m: (m, 0)), spec(shift_chunk), spec(shift_chunk + 1)],
        out_specs=pl.BlockSpec((tm, d), lambda m: (m, 0)),
        out_shape=jax.ShapeDtypeStruct((mrows, d), BF16),
        compiler_params=_params("arbitrary"),
        name="modulate",
    )(x, mod.arr, mod.arr)


def _final_norm_kernel(x_ref, g_ref, o_ref):
    x = x_ref[...]
    r = lax.rsqrt(jnp.mean(x * x, axis=-1, keepdims=True) + RMS_EPS)
    o_ref[...] = (x * r) * g_ref[...]


def _final_norm(x, g):
    mrows, d = x.shape
    tm = _row_tile(mrows, 256)
    return pl.pallas_call(
        _final_norm_kernel,
        grid=(mrows // tm,),
        in_specs=[pl.BlockSpec((tm, d), lambda m: (m, 0)), pl.BlockSpec((1, d), lambda m: (0, 0))],
        out_specs=pl.BlockSpec((tm, d), lambda m: (m, 0)),
        out_shape=jax.ShapeDtypeStruct((mrows, d), F32),
        compiler_params=_params("arbitrary"),
        name="final_norm",
    )(x, g.reshape(1, d))


def _bf16_tiles(m, *pairs):
    todo = [(w_ref, keep_ref) for w_ref, keep_ref in pairs if w_ref.dtype != BF16]
    if todo:
        @pl.when(m == 0)
        def _():
            for w_ref, keep_ref in todo:
                keep_ref[...] = w_ref[...].astype(BF16)

    return [w_ref if w_ref.dtype == BF16 else keep_ref for w_ref, keep_ref in pairs]


def _weight_spec(w, layer, k, tn, index, **mode):
    if w.dtype == BF16:
        return pl.BlockSpec((k, tn), lambda j, m: (0, index(j)), **mode)
    return pl.BlockSpec((None, k, tn), lambda j, m: (layer, 0, index(j)), **mode)


def _mm_kernel(*refs, has_bias, has_res, tm, rs, per_row):
    it = iter(refs)
    x_ref, w_ref = next(it), next(it)
    b_ref = next(it) if has_bias else None
    res_ref, g_ref = (next(it), next(it)) if has_res else (None, None)
    o_ref, keep_ref = next(it), next(it, None)
    m = pl.program_id(1)
    (rhs_ref,) = _bf16_tiles(m, (w_ref, keep_ref))
    acc = jnp.dot(x_ref[...], rhs_ref[...], preferred_element_type=F32)
    if has_bias:
        acc = acc + b_ref[...]
    if has_res:
        acc = res_ref[...] + _mod_value(g_ref, m, tm, rs, per_row) * acc
    if len(o_ref.shape) == 3:
        for c in range(o_ref.shape[0]):
            o_ref[c] = acc[:, c * HEAD_DIM:(c + 1) * HEAD_DIM].astype(o_ref.dtype)
    else:
        o_ref[...] = acc.astype(o_ref.dtype)


def _mm(x, w, layer, *, out_dtype, bias=None, res=None, gate=None, tm_pref=1024, tn_pref=512,
        head_major=False, single_buffer_w=False, emit_bf16=False):
    mrows, k = x.shape
    n = w.shape[-1]
    tm = _row_tile(mrows, tm_pref)
    tn = _col_tile(n, tn_pref)
    w_mode = dict(pipeline_mode=pl.Buffered(1)) if single_buffer_w else {}
    in_specs = [pl.BlockSpec((tm, k), lambda j, m: (m, 0)),
                _weight_spec(w, layer, k, tn, lambda j: j, **w_mode)]
    args = [x, w]
    if bias is not None:
        in_specs.append(pl.BlockSpec((None, 1, tn), lambda j, m: (layer, 0, j)))
        args.append(bias.reshape(bias.shape[0], 1, n))
    rs, per_row = 1, True
    if res is not None:
        mod, gl, gate_chunk = gate
        rs, per_row = mod.rs, mod.per_row
        goff = gate_chunk * (n // tn)
        in_specs.append(pl.BlockSpec((tm, tn), lambda j, m: (m, j)))
        if per_row:
            in_specs.append(pl.BlockSpec((None, tm, tn), lambda j, m: (gl, m, goff + j)))
        else:
            in_specs.append(pl.BlockSpec((None, ADA_ROWS, tn), lambda j, m: (gl, 0, goff + j)))
        args += [res, mod.arr]
    if head_major:
        out_spec = pl.BlockSpec((tn // HEAD_DIM, tm, HEAD_DIM), lambda j, m: (j, m, 0))
        out_shape = jax.ShapeDtypeStruct((n // HEAD_DIM, mrows, HEAD_DIM), out_dtype)
    else:
        out_spec = pl.BlockSpec((tm, tn), lambda j, m: (m, j))
        out_shape = jax.ShapeDtypeStruct((mrows, n), out_dtype)
    scratch = []
    if emit_bf16:
        out_spec = (out_spec, pl.BlockSpec((k, tn), lambda j, m: (0, j)))
        out_shape = (out_shape, jax.ShapeDtypeStruct((k, n), BF16))
    elif w.dtype != BF16:
        scratch = [pltpu.VMEM((k, tn), BF16)]
    return pl.pallas_call(
        functools.partial(_mm_kernel, has_bias=bias is not None, has_res=res is not None,
                          tm=tm, rs=rs, per_row=per_row),
        grid=(n // tn, mrows // tm),
        in_specs=in_specs,
        out_specs=out_spec,
        out_shape=out_shape,
        scratch_shapes=scratch,
        compiler_params=_params("arbitrary", "arbitrary"),
        name="matmul",
    )(*args)


def _causal_taps(u, dw_ref, row, p0, p1):
    u1 = jnp.where(row == 0, p1, pltpu.roll(u, 1, 0))
    u2 = jnp.where(row == 0, p0, jnp.where(row == 1, p1, pltpu.roll(u, 2, 0)))
    return dw_ref[0:1, :] * u2 + dw_ref[1:2, :] * u1 + dw_ref[2:3, :] * u


def _pair_kernel(*refs, mode, rs):
    it = iter(refs)
    x_ref, wa_ref, wb_ref = next(it), next(it), next(it)
    if mode == "glu":
        ba_ref, bb_ref = next(it), next(it)
    else:
        dwa_ref, dwb_ref = next(it), next(it)
        c0a_ref, c0b_ref, c1a_ref, c1b_ref = next(it), next(it), next(it), next(it)
    o_ref = next(it)
    if mode == "ffn":
        ta_ref, tb_ref = next(it), next(it)
    keep_a_ref, keep_b_ref = next(it, None), next(it, None)
    m = pl.program_id(1)
    rhs_a_ref, rhs_b_ref = _bf16_tiles(m, (wa_ref, keep_a_ref), (wb_ref, keep_b_ref))
    x = x_ref[...]
    ua = jnp.dot(x, rhs_a_ref[...], preferred_element_type=F32)
    ub = jnp.dot(x, rhs_b_ref[...], preferred_element_type=F32)

    if mode == "glu":
        o_ref[...] = ((ua + ba_ref[...]) * jax.nn.sigmoid(ub + bb_ref[...])).astype(o_ref.dtype)
        return

    assert rs & (rs - 1) == 0
    row = lax.broadcasted_iota(jnp.int32, ua.shape, 0) & (rs - 1)
    gate = _causal_taps(ua, dwa_ref, row, c0a_ref[...], c1a_ref[...])
    val = _causal_taps(ub, dwb_ref, row, c0b_ref[...], c1b_ref[...])
    ta_ref[...] = ua
    tb_ref[...] = ub
    o_ref[...] = ((gate * jax.nn.sigmoid(gate)) * val).astype(o_ref.dtype)


def _pair_weights(w, layer, k, half, tn):
    if isinstance(w, tuple):
        return [_weight_spec(w[0], layer, k, tn, lambda j: j), _weight_spec(w[1], layer, k, tn, lambda j: j)], list(w)
    hoff = half // tn
    return [_weight_spec(w, layer, k, tn, lambda j: j), _weight_spec(w, layer, k, tn, lambda j: hoff + j)], [w, w]


def _pair_mm(x, w, layer, *, mode, rs, bias=None, dw=None, ctx=None, tm_pref=1024, tn_pref=256, emit_bf16=False):
    mrows, k = x.shape
    rounded = isinstance(w, tuple)
    half = w[0].shape[1] if rounded else w.shape[2] // 2
    n2 = 2 * half
    tm = _row_tile(mrows, tm_pref)
    tn = _col_tile(half, tn_pref)
    hoff = half // tn
    assert mode == "glu" or tm % rs == 0
    w_specs, w_args = _pair_weights(w, layer, k, half, tn)
    in_specs = [pl.BlockSpec((tm, k), lambda j, m: (m, 0))] + w_specs
    args = [x] + w_args
    scratch = [] if rounded or emit_bf16 else [pltpu.VMEM((k, tn), BF16), pltpu.VMEM((k, tn), BF16)]
    if mode == "glu":
        b3 = bias.reshape(bias.shape[0], 1, n2)
        in_specs += [pl.BlockSpec((None, 1, tn), lambda j, m: (layer, 0, j)),
                     pl.BlockSpec((None, 1, tn), lambda j, m: (layer, 0, hoff + j))]
        args += [b3, b3]
        out_shape = jax.ShapeDtypeStruct((mrows, half), F32)
        out_specs = pl.BlockSpec((tm, tn), lambda j, m: (m, j))
        if emit_bf16:
            out_shape = (out_shape,) + (jax.ShapeDtypeStruct((k, half), BF16),) * 2
            out_specs = (out_specs,) + (pl.BlockSpec((k, tn), lambda j, m: (0, j)),) * 2
    else:
        taps = dw.shape[1]
        in_specs += [pl.BlockSpec((None, taps, tn), lambda j, m: (layer, 0, j)),
                     pl.BlockSpec((None, taps, tn), lambda j, m: (layer, 0, hoff + j))]
        args += [dw, dw]
        c0, c1 = ctx
        lo = pl.BlockSpec((tm, tn), lambda j, m: (m, j))
        hi = pl.BlockSpec((tm, tn), lambda j, m: (m, hoff + j))
        in_specs += [lo, hi, lo, hi]
        args += [c0, c0, c1, c1]
        out_shape = (jax.ShapeDtypeStruct((mrows, half), BF16),) + (jax.ShapeDtypeStruct((mrows, half), F32),) * 2
        out_specs = (lo, lo, lo)
    return pl.pallas_call(
        functools.partial(_pair_kernel, mode=mode, rs=rs),
        grid=(half // tn, mrows // tm),
        in_specs=in_specs,
        out_specs=out_specs,
        out_shape=out_shape,
        scratch_shapes=scratch,
        compiler_params=_params("arbitrary", "arbitrary"),
        name="pair_" + mode,
    )(*args)


_FFN_CHUNK_ROWS = 128


def _ffn_up_kernel(x_ref, wa_ref, wb_ref, dwa_ref, dwb_ref, o_ref, ta_ref, tb_ref,
                   wabf_ref, wbbf_ref, ua_ref, ub_ref, *, tm, rs):
    m = pl.program_id(1)
    rc = min(_FFN_CHUNK_ROWS, tm)

    _bf16_tiles(m, (wa_ref, wabf_ref), (wb_ref, wbbf_ref))

    @pl.when((m * tm) % rs == 0)
    def _():
        ua_ref[0:8, :] = jnp.zeros((8, ua_ref.shape[1]), F32)
        ub_ref[0:8, :] = jnp.zeros((8, ub_ref.shape[1]), F32)

    x = x_ref[...]
    ua_ref[8:tm + 8, :] = jnp.dot(x, wabf_ref[...], preferred_element_type=F32)
    ub_ref[8:tm + 8, :] = jnp.dot(x, wbbf_ref[...], preferred_element_type=F32)

    def chunk(c, carry):
        r0 = pl.multiple_of(c * rc, rc)

        def taps(u_ref, dw_ref):
            u = u_ref[pl.ds(r0, rc + 8), :]
            y = dw_ref[0:1, :] * pltpu.roll(u, 2, 0) + dw_ref[1:2, :] * pltpu.roll(u, 1, 0) + dw_ref[2:3, :] * u
            return y[8:, :]

        gate = taps(ua_ref, dwa_ref)
        val = taps(ub_ref, dwb_ref)
        o_ref[pl.ds(r0, rc), :] = ((gate * jax.nn.sigmoid(gate)) * val).astype(o_ref.dtype)
        return carry

    lax.fori_loop(0, tm // rc, chunk, 0)

    @pl.when(((m + 1) * tm) % rs == 0)
    def _():
        ta_ref[...] = ua_ref[tm:tm + 8, :]
        tb_ref[...] = ub_ref[tm:tm + 8, :]

    ua_ref[0:8, :] = ua_ref[tm:tm + 8, :]
    ub_ref[0:8, :] = ub_ref[tm:tm + 8, :]


def _ffn_up(x, w, layer, dw, *, rs, tm_pref=1024, tn_pref=256):
    mrows, k = x.shape
    half = w.shape[2] // 2
    taps = dw.shape[1]
    tm = _row_tile(mrows, tm_pref)
    tn = _col_tile(half, tn_pref)
    hoff = half // tn
    assert rs % tm == 0 and taps == 3
    tail_shape = jax.ShapeDtypeStruct((mrows // rs, 8, half), F32)
    tail_spec = pl.BlockSpec((None, 8, tn), lambda j, m: ((m * tm) // rs, 0, j))
    wbf_shape = jax.ShapeDtypeStruct((k, half), BF16)
    wbf_spec = pl.BlockSpec((k, tn), lambda j, m: (0, j))
    return pl.pallas_call(
        functools.partial(_ffn_up_kernel, tm=tm, rs=rs),
        grid=(half // tn, mrows // tm),
        in_specs=[pl.BlockSpec((tm, k), lambda j, m: (m, 0)),
                  pl.BlockSpec((None, k, tn), lambda j, m: (layer, 0, j)),
                  pl.BlockSpec((None, k, tn), lambda j, m: (layer, 0, hoff + j)),
                  pl.BlockSpec((None, taps, tn), lambda j, m: (layer, 0, j)),
                  pl.BlockSpec((None, taps, tn), lambda j, m: (layer, 0, hoff + j))],
        out_specs=(pl.BlockSpec((tm, tn), lambda j, m: (m, j)), tail_spec, tail_spec, wbf_spec, wbf_spec),
        out_shape=(jax.ShapeDtypeStruct((mrows, half), BF16), tail_shape, tail_shape, wbf_shape, wbf_shape),
        scratch_shapes=[pltpu.VMEM((tm + 8, tn), F32)] * 2,
        compiler_params=_params("arbitrary", "arbitrary"),
        name="ffn_up",
    )(x, w, w, dw, dw)


def _band_bias(n_keys, n_steps):
    qi = lax.broadcasted_iota(jnp.int32, (Q_BLOCK, n_keys), 0)
    ki = lax.broadcasted_iota(jnp.int32, (Q_BLOCK, n_keys), 1)
    dist = qi + (n_keys - Q_BLOCK) - ki
    return jnp.where((dist >= 0) & (dist <= n_steps), 0.0, -jnp.inf).astype(F32)


def _attn_prompt_kernel(*refs, seq):
    qkv_refs, out_ref, scr = refs[:3 * N_GROUPS], refs[3 * N_GROUPS], refs[3 * N_GROUPS + 1:]
    scale = HEAD_DIM ** -0.5
    nt = (((1,), (1,)), ((), ()))
    for g, (window, dil) in enumerate(DILATED_GROUPS):
        q_ref, k_ref, v_ref = qkv_refs[3 * g:3 * g + 3]
        o_scr, l_scr = scr[2 * g], scr[2 * g + 1]
        n_steps = window // dil
        bias_first = _band_bias(Q_BLOCK, n_steps)
        bias_full = _band_bias(2 * Q_BLOCK, n_steps)
        for r in range(dil):
            for blk in range(seq // dil // Q_BLOCK):
                start = r + dil * Q_BLOCK * blk

                def rows(first, count):
                    return pl.ds(first, count, stride=dil) if dil > 1 else pl.ds(first, count)

                q = q_ref[rows(start, Q_BLOCK), :].astype(BF16)
                if blk == 0:
                    keys, bias = rows(start, Q_BLOCK), bias_first
                else:
                    keys, bias = rows(start - dil * Q_BLOCK, 2 * Q_BLOCK), bias_full
                k = k_ref[keys, :].astype(BF16)
                v = v_ref[keys, :].astype(BF16)
                s = lax.dot_general(q, k, nt, preferred_element_type=F32) * scale + bias
                mx = jnp.max(s, axis=-1, keepdims=True)
                p = jnp.exp(s - mx)
                den = jnp.sum(p, axis=-1, keepdims=True)
                o_scr[rows(start, Q_BLOCK), :] = jnp.dot(p.astype(BF16), v, preferred_element_type=F32) / den
                l_scr[rows(start, Q_BLOCK), :] = jnp.broadcast_to(mx + jnp.log(den), (Q_BLOCK, HEAD_DIM))
    l1, l2, l3 = scr[1][...], scr[3][...], scr[5][...]
    mx = jnp.maximum(jnp.maximum(l1, l2), l3)
    e1, e2, e3 = jnp.exp(l1 - mx), jnp.exp(l2 - mx), jnp.exp(l3 - mx)
    num = e1 * scr[0][...] + e2 * scr[2][...] + e3 * scr[4][...]
    out_ref[...] = (num / (e1 + e2 + e3)).astype(out_ref.dtype)


def _attn_prompt(qkv_t, nseq, seq):
    heads = qkv_t.shape[0] // (3 * N_GROUPS)
    for window, dil in DILATED_GROUPS:
        assert seq % (dil * Q_BLOCK) == 0 and window // dil <= Q_BLOCK
    in_specs = [pl.BlockSpec((None, seq, HEAD_DIM), lambda b, h, c=c: (c * heads + h, b, 0))
                for c in range(3 * N_GROUPS)]
    return pl.pallas_call(
        functools.partial(_attn_prompt_kernel, seq=seq),
        grid=(nseq, heads),
        in_specs=in_specs,
        out_specs=pl.BlockSpec((seq, HEAD_DIM), lambda b, h: (b, h)),
        out_shape=jax.ShapeDtypeStruct((nseq * seq, heads * HEAD_DIM), BF16),
        scratch_shapes=[pltpu.VMEM((seq, HEAD_DIM), F32)] * (2 * N_GROUPS),
        compiler_params=_params("arbitrary", "arbitrary"),
        name="attn_prompt",
    )(*([qkv_t] * (3 * N_GROUPS)))


def _attn_sample_kernel(*refs, heads, steps):
    qkv_refs, cache_refs, o_ref = refs[:3 * N_GROUPS], refs[3 * N_GROUPS:4 * N_GROUPS], refs[4 * N_GROUPS]
    assert heads & (heads - 1) == 0
    t = pl.program_id(1)
    scale = HEAD_DIM ** -0.5
    nt = (((1,), (1,)), ((), ()))
    log_heads = jnp.int32(heads.bit_length() - 1)

    def columns(n_pos):
        col = lax.broadcasted_iota(jnp.int32, (heads, n_pos * heads), 1)
        row = lax.broadcasted_iota(jnp.int32, (heads, n_pos * heads), 0)
        return (col & (heads - 1)) == row, lax.shift_right_logical(col, log_heads)

    own_c, pos_c = columns(Q_BLOCK)
    own_n, pos_n = columns(steps)
    scores, values = [], []
    for g, (_, dil) in enumerate(DILATED_GROUPS):
        q_ref, kn_ref, vn_ref = qkv_refs[3 * g:3 * g + 3]
        c_ref = cache_refs[g]
        q = q_ref[...].astype(BF16)
        kc = c_ref[:, 0:heads, :].reshape(Q_BLOCK * heads, HEAD_DIM).astype(BF16)
        kn = kn_ref[...].reshape(steps * heads, HEAD_DIM).astype(BF16)
        sc = lax.dot_general(q, kc, nt, preferred_element_type=F32) * scale
        sn = lax.dot_general(q, kn, nt, preferred_element_type=F32) * scale
        sc = jnp.where(own_c & (pos_c >= t // dil), sc, -jnp.inf)
        back = t - pos_n
        sn = jnp.where(own_n & (back >= 0) & ((back & (dil - 1)) == 0), sn, -jnp.inf)
        scores += [sc, sn]
        values += [c_ref[:, heads:2 * heads, :].reshape(Q_BLOCK * heads, HEAD_DIM).astype(BF16),
                   vn_ref[...].reshape(steps * heads, HEAD_DIM).astype(BF16)]
    mx = functools.reduce(jnp.maximum, [jnp.max(s, axis=-1, keepdims=True) for s in scores])
    den = jnp.zeros((heads, 1), F32)
    acc = jnp.zeros((heads, HEAD_DIM), F32)
    for s, v in zip(scores, values):
        p = jnp.exp(s - mx)
        den = den + jnp.sum(p, axis=-1, keepdims=True)
        acc = acc + jnp.dot(p.astype(BF16), v, preferred_element_type=F32)
    o_ref[...] = acc / den


def _attn_sample(qkv, caches, layer, nseq, steps):
    aw = qkv.shape[1] // (3 * N_GROUPS)
    heads = aw // HEAD_DIM
    qkv4 = qkv.reshape(nseq, steps, 3 * N_GROUPS * heads, HEAD_DIM)
    in_specs, args = [], []
    for g in range(N_GROUPS):
        in_specs += [pl.BlockSpec((None, None, heads, HEAD_DIM), lambda b, t, g=g: (b, t, 3 * g, 0)),
                     pl.BlockSpec((None, steps, heads, HEAD_DIM), lambda b, t, g=g: (b, 0, 3 * g + 1, 0)),
                     pl.BlockSpec((None, steps, heads, HEAD_DIM), lambda b, t, g=g: (b, 0, 3 * g + 2, 0))]
        args += [qkv4] * 3
    for (window, dil), cache in zip(DILATED_GROUPS, caches):
        assert cache.shape[2] == window and window // dil == Q_BLOCK
        args.append(cache.reshape(cache.shape[0] * nseq, Q_BLOCK, dil, 2 * heads, HEAD_DIM))
        in_specs.append(pl.BlockSpec((None, Q_BLOCK, None, 2 * heads, HEAD_DIM),
                                     lambda b, t, dil=dil: (layer * nseq + b, 0, t % dil, 0, 0)))
    out = pl.pallas_call(
        functools.partial(_attn_sample_kernel, heads=heads, steps=steps),
        grid=(nseq, steps),
        in_specs=in_specs,
        out_specs=pl.BlockSpec((None, None, heads, HEAD_DIM), lambda b, t: (b, t, 0, 0)),
        out_shape=jax.ShapeDtypeStruct((nseq, steps, heads, HEAD_DIM), F32),
        compiler_params=_params("arbitrary", "arbitrary"),
        name="attn_sample",
    )(*args)
    return out.reshape(nseq * steps, aw)


_SLIDE_BLOCK_BYTES = 8 * 1024 * 1024


def _slide_kernel(main_ref, next_ref, new_ref, o_ref, *, groups):
    i = pl.program_id(1)
    last = pl.num_programs(1) - 1
    if groups > 1:
        o_ref[0:groups - 1] = main_ref[1:groups]

    @pl.when(i < last)
    def _():
        o_ref[groups - 1] = next_ref[0]

    @pl.when(i == last)
    def _():
        o_ref[groups - 1] = new_ref[...]


def _slide_cache(cache, layer, new_rows):
    layers, nseq, window = cache.shape[:3]
    steps = new_rows.shape[1]
    rows = steps * cache.shape[3] * cache.shape[4]
    assert window % steps == 0 and rows % 8 == 0
    ngroups = window // steps
    groups = _row_tile(ngroups, max(1, _SLIDE_BLOCK_BYTES // (rows * HEAD_DIM * 4)))
    view = cache.reshape(layers * nseq, ngroups, rows, HEAD_DIM)
    blk = (None, groups, rows, HEAD_DIM)
    out = pl.pallas_call(
        functools.partial(_slide_kernel, groups=groups),
        grid=(nseq, ngroups // groups),
        in_specs=[pl.BlockSpec(blk, lambda b, i: (layer * nseq + b, i, 0, 0)),
                  pl.BlockSpec((None, 1, rows, HEAD_DIM),
                               lambda b, i: (layer * nseq + b, jnp.minimum((i + 1) * groups, ngroups - 1), 0, 0)),
                  pl.BlockSpec((None, rows, HEAD_DIM), lambda b, i: (b, 0, 0))],
        out_specs=pl.BlockSpec(blk, lambda b, i: (b, i, 0, 0)),
        out_shape=jax.ShapeDtypeStruct((nseq, ngroups, rows, HEAD_DIM), F32),
        compiler_params=_params("arbitrary", "arbitrary"),
        name="cache_slide",
    )(view, view, new_rows.reshape(nseq, rows, HEAD_DIM))
    return out.reshape((nseq,) + cache.shape[2:])


_CONV_ROWS = 64
_CONV_COLS = 256


def _dwln_kernel(u_ref, ctx_ref, w_ref, bdw_ref, g_ref, b_ref, o_ref, ext_ref, y_ref, *, tm, rs, taps, zero_start):
    m = pl.program_id(0)
    if zero_start:
        at_start = (m * tm) % rs == 0

        @pl.when(at_start)
        def _():
            ext_ref[0:CONV_CTX_ROWS, :] = jnp.zeros((CONV_CTX_ROWS, ext_ref.shape[1]), F32)

        @pl.when(jnp.logical_not(at_start))
        def _():
            ext_ref[0:CONV_CTX_ROWS, :] = ctx_ref[...]
    else:
        ext_ref[0:CONV_CTX_ROWS, :] = ctx_ref[...]
    ext_ref[CONV_CTX_ROWS:CONV_CTX_ROWS + tm, :] = u_ref[...]
    ext_ref[CONV_CTX_ROWS + tm:CONV_CTX_ROWS + tm + 8, :] = jnp.zeros((8, ext_ref.shape[1]), F32)

    d = u_ref.shape[1]
    rc = min(_CONV_ROWS, tm)
    cw = min(_CONV_COLS, d)
    first = CONV_CTX_ROWS - (taps - 1)
    for r0 in range(0, tm, rc):
        for c0 in range(0, d, cw):
            acc = jnp.zeros((rc, cw), F32)
            for s in range(8):
                part = None
                for k in range(taps):
                    if (first + k) % 8 != s:
                        continue
                    a = r0 + first + k - s
                    term = w_ref[k:k + 1, c0:c0 + cw] * ext_ref[a:a + rc + 8, c0:c0 + cw]
                    part = term if part is None else part + term
                if part is not None:
                    acc = acc + part[s:s + rc, :]
            y_ref[r0:r0 + rc, c0:c0 + cw] = acc + bdw_ref[:, c0:c0 + cw]

    y = y_ref[...]
    mu = jnp.mean(y, axis=-1, keepdims=True)
    yc = y - mu
    var = jnp.mean(yc * yc, axis=-1, keepdims=True)
    z = yc * lax.rsqrt(var + LN_EPS) * g_ref[...] + b_ref[...]
    o_ref[...] = (z * jax.nn.sigmoid(z)).astype(o_ref.dtype)


def _dwln(u, ctx, layer, w_dw, b_dw, ln_g, ln_b, *, rs, tm):
    mrows, d = u.shape
    taps = w_dw.shape[1]
    assert taps - 1 <= CONV_CTX_ROWS and tm % 8 == 0
    wpad = jnp.pad(w_dw, ((0, 0), (0, CONV_CTX_ROWS - taps), (0, 0)))
    zero_start = ctx is None
    if zero_start:
        assert tm % CONV_CTX_ROWS == 0
        per = tm // CONV_CTX_ROWS
        ctx_arr = u
        ctx_spec = pl.BlockSpec((CONV_CTX_ROWS, d), lambda m: (jnp.maximum(m * per - 1, 0), 0))
    else:
        assert tm == rs
        ctx_arr = ctx
        ctx_spec = pl.BlockSpec((None, CONV_CTX_ROWS, d), lambda m: (m, 0, 0))
    vec = lambda a: a.reshape(a.shape[0], 1, d)
    vec_spec = pl.BlockSpec((None, 1, d), lambda m: (layer, 0, 0))
    return pl.pallas_call(
        functools.partial(_dwln_kernel, tm=tm, rs=rs, taps=taps, zero_start=zero_start),
        grid=(mrows // tm,),
        in_specs=[pl.BlockSpec((tm, d), lambda m: (m, 0)), ctx_spec,
                  pl.BlockSpec((None, CONV_CTX_ROWS, d), lambda m: (layer, 0, 0)),
                  vec_spec, vec_spec, vec_spec],
        out_specs=pl.BlockSpec((tm, d), lambda m: (m, 0)),
        out_shape=jax.ShapeDtypeStruct((mrows, d), BF16),
        scratch_shapes=[pltpu.VMEM((CONV_CTX_ROWS + tm + 8, d), F32), pltpu.VMEM((tm, d), F32)],
        compiler_params=_params("arbitrary"),
        name="dwconv_ln",
    )(u, ctx_arr, wpad, vec(b_dw), vec(ln_g), vec(ln_b))


def _trunk(x, mod, nseq, rs, params, kv_caches, conv_states, ffn_states, rounded):
    (w_qkv, w_o, w_pw1, b_pw1, w_dw, b_dw, ln_g, ln_b, w_pw2, b_pw2,
     w_up, w_ffn_dw, w_down, final_norm_g) = params
    depth = w_up.shape[0]
    d = x.shape[1]
    prompt = kv_caches is None
    new_kv = [[] for _ in range(N_GROUPS)]
    new_conv, new_ffn = [], []

    def gated(name, xin, w, layer, **kw):
        if prompt:
            out, rounded[name, layer] = _mm(xin, w, layer, out_dtype=F32, emit_bf16=True, **kw)
            return out
        return _mm(xin, rounded[name, layer], layer, out_dtype=F32, **kw)

    for i in range(depth):
        j = i // 2
        h = _modulate(x, mod, i, 0)
        if i % 2 == 0:
            heads = w_qkv.shape[2] // (3 * N_GROUPS * HEAD_DIM)
            if prompt:
                qkv_t, rounded["qkv", j] = _mm(h, w_qkv, j, out_dtype=F32, head_major=True, emit_bf16=True)
                mixed = _attn_prompt(qkv_t, nseq, rs)
                qkv5 = qkv_t.reshape(N_GROUPS, 3, heads, nseq, rs, HEAD_DIM)
                for g, (window, _) in enumerate(DILATED_GROUPS):
                    keep = min(window, rs)
                    new_kv[g].append(jnp.transpose(qkv5[g, 1:, :, :, rs - keep:], (2, 3, 0, 1, 4)))
            else:
                qkv = _mm(h, rounded["qkv", j], j, out_dtype=F32)
                mixed = _attn_sample(qkv, kv_caches, j, nseq, rs).astype(BF16)
                qkv5 = qkv.reshape(nseq, rs, N_GROUPS, 3, heads, HEAD_DIM)
                for g in range(N_GROUPS):
                    new_kv[g].append(_slide_cache(kv_caches[g], j, qkv5[:, :, g, 1:]))
            x = gated("o", mixed, w_o, j, res=x, gate=(mod, i, 2))
        else:
            if prompt:
                u, *rounded["pw1", j] = _pair_mm(h, w_pw1, j, mode="glu", rs=rs, bias=b_pw1, emit_bf16=True)
            else:
                u = _pair_mm(h, tuple(rounded["pw1", j]), j, mode="glu", rs=rs, bias=b_pw1)
            taps = w_dw.shape[1]
            if prompt:
                y = _dwln(u, None, j, w_dw, b_dw, ln_g, ln_b, rs=rs, tm=128)
                new_conv.append(u.reshape(nseq, rs, d)[:, rs - (taps - 1):])
            else:
                state = conv_states[j]
                ctx = jnp.pad(state, ((0, 0), (CONV_CTX_ROWS - (taps - 1), 0), (0, 0)))
                y = _dwln(u, ctx, j, w_dw, b_dw, ln_g, ln_b, rs=rs, tm=rs)
                u_all = jnp.concatenate([state, u.reshape(nseq, rs, d)], axis=1)
                new_conv.append(u_all[:, -(taps - 1):])
            x = gated("pw2", y, w_pw2, j, bias=b_pw2, res=x, gate=(mod, i, 2))
        h = _modulate(x, mod, i, 3)
        if prompt:
            act, ta, tb, *rounded["up", i] = _ffn_up(h, w_up, i, w_ffn_dw, rs=rs)
            new_ffn.append(jnp.concatenate([ta[:, 6:8], tb[:, 6:8]], axis=-1))
        else:
            state = ffn_states[i]
            c0 = jnp.repeat(state[:, 0], rs, axis=0)
            c1 = jnp.repeat(state[:, 1], rs, axis=0)
            act, ta, tb = _pair_mm(h, tuple(rounded["up", i]), i, mode="ffn", rs=rs, dw=w_ffn_dw, ctx=(c0, c1))
            u_ffn = jnp.concatenate([ta, tb], axis=-1).reshape(nseq, rs, -1)
            new_ffn.append(jnp.concatenate([state, u_ffn], axis=1)[:, -state.shape[1]:])
        x = _mm(act, w_down, i, out_dtype=F32, res=x, gate=(mod, i, 5), tm_pref=256, tn_pref=512,
                single_buffer_w=True)
    y = _final_norm(x, final_norm_g)
    return y, [jnp.stack(l) for l in new_kv], jnp.stack(new_conv), jnp.stack(new_ffn)


def kernel(x_prompt, x_sample, c_prompt, c_sample, cache_kv_g1, cache_kv_g2, cache_kv_g3, state_conv, state_ffn_conv, w_ada, b_ada, w_qkv, w_o, w_pw1, b_pw1, w_dw, b_dw, ln_g, ln_b, w_pw2, b_pw2, w_up, w_ffn_dw, w_down, final_norm_g):
    nb, seq, d = x_prompt.shape
    ndb, steps, _ = x_sample.shape
    assert nb + ndb <= ADA_ROWS
    params = (w_qkv, w_o, w_pw1, b_pw1, w_dw, b_dw, ln_g, ln_b, w_pw2, b_pw2,
              w_up, w_ffn_dw, w_down, final_norm_g)

    c_rows = jnp.concatenate([c_prompt, c_sample, jnp.zeros((ADA_ROWS - nb - ndb, d), F32)], axis=0)
    ada = _adaln(c_rows, w_ada, b_ada)
    mod_p = _Mod(ada, False, seq)
    mod_s = _Mod(jnp.repeat(ada[:, nb:nb + ndb], steps, axis=1), True, steps)

    rounded = {}
    y_p, kv_p, conv_p, ffn_p = _trunk(x_prompt.reshape(nb * seq, d), mod_p, nb, seq, params, None, None, None,
                                      rounded)
    y_s, kv_s, conv_s, ffn_s = _trunk(x_sample.reshape(ndb * steps, d), mod_s, ndb, steps, params,
                                      (cache_kv_g1, cache_kv_g2, cache_kv_g3), state_conv, state_ffn_conv, rounded)
    return (y_p.reshape(nb, seq, d), y_s.reshape(ndb, steps, d), kv_p[0], kv_p[1], kv_p[2], conv_p, ffn_p,
            kv_s[0], kv_s[1], kv_s[2], conv_s, ffn_s)
```

```python
import functools

import jax
import jax.numpy as jnp
from jax import lax
from jax.experimental import pallas as pl
from jax.experimental.pallas import tpu as pltpu

F32 = jnp.float32
BF16 = jnp.bfloat16

HEAD_DIM = 128
Q_BLOCK = 128
DILATED_GROUPS = ((128, 1), (512, 4), (2048, 16))
N_GROUPS = len(DILATED_GROUPS)
RMS_EPS = 1e-6
LN_EPS = 1e-5
ADA_ROWS = 16
CONV_CTX_ROWS = 32
V7X_VMEM_LIMIT_BYTES = 56 * 1024 * 1024


def _params(*sem):
    return pltpu.CompilerParams(dimension_semantics=sem, vmem_limit_bytes=V7X_VMEM_LIMIT_BYTES)


def _col_tile(n, pref):
    if n <= pref:
        return n
    t = pref - pref % 128
    while t >= 128:
        if n % t == 0:
            return t
        t -= 128
    raise ValueError(f"no lane-aligned tile for {n}")


def _row_tile(m, pref):
    t = 1
    while t * 2 <= pref and m % (t * 2) == 0:
        t *= 2
    return t


class _Mod:
    def __init__(self, arr, per_row, rows_per_seq):
        self.arr, self.per_row, self.rs = arr, per_row, rows_per_seq


def _mod_value(ref, m, tm, rs, per_row):
    if per_row:
        return ref[...]
    return ref[pl.ds((m * tm) // rs, 1), :]


def _adaln_kernel(c_ref, w_ref, b_ref, o_ref):
    c = c_ref[...]
    s = (c * jax.nn.sigmoid(c)).astype(BF16)
    o_ref[...] = jnp.dot(s, w_ref[...].astype(BF16), preferred_element_type=F32) + b_ref[...]


def _adaln(c_rows, w_ada, b_ada):
    depth, d, n = w_ada.shape
    tn = _col_tile(n, 512)
    return pl.pallas_call(
        _adaln_kernel,
        grid=(depth, n // tn),
        in_specs=[pl.BlockSpec((ADA_ROWS, d), lambda i, j: (0, 0)),
                  pl.BlockSpec((None, d, tn), lambda i, j: (i, 0, j)),
                  pl.BlockSpec((None, 1, tn), lambda i, j: (i, 0, j))],
        out_specs=pl.BlockSpec((None, ADA_ROWS, tn), lambda i, j: (i, 0, j)),
        out_shape=jax.ShapeDtypeStruct((depth, ADA_ROWS, n), F32),
        compiler_params=_params("arbitrary", "arbitrary"),
        name="adaln",
    )(c_rows, w_ada, b_ada.reshape(depth, 1, n))


def _modulate_kernel(x_ref, sh_ref, sc_ref, o_ref, *, tm, rs, per_row):
    m = pl.program_id(0)
    x = x_ref[...]
    r = lax.rsqrt(jnp.mean(x * x, axis=-1, keepdims=True) + RMS_EPS)
    sh = _mod_value(sh_ref, m, tm, rs, per_row)
    sc = _mod_value(sc_ref, m, tm, rs, per_row)
    o_ref[...] = ((x * r) * (1.0 + sc) + sh).astype(o_ref.dtype)


def _modulate(x, mod, layer, shift_chunk):
    mrows, d = x.shape
    tm = _row_tile(mrows, 256)
    if mod.per_row:
        spec = lambda c: pl.BlockSpec((None, tm, d), lambda m: (layer, m, c))
    else:
        spec = lambda c: pl.BlockSpec((None, ADA_ROWS, d), lambda m: (layer, 0, c))
    return pl.pallas_call(
        functools.partial(_modulate_kernel, tm=tm, rs=mod.rs, per_row=mod.per_row),
        grid=(mrows // tm,),
        in_specs=[pl.BlockSpec((tm, d), lambda m: (m, 0)), spec(shift_chunk), spec(shift_chunk + 1)],
        out_specs=pl.BlockSpec((tm, d), lambda m: (m, 0)),
        out_shape=jax.ShapeDtypeStruct((mrows, d), BF16),
        compiler_params=_params("arbitrary"),
        name="modulate",
    )(x, mod.arr, mod.arr)


def _final_norm_kernel(x_ref, g_ref, o_ref):
    x = x_ref[...]
    r = lax.rsqrt(jnp.mean(x * x, axis=-1, keepdims=True) + RMS_EPS)
    o_ref[...] = (x * r) * g_ref[...]


def _final_norm(x, g):
    mrows, d = x.shape
    tm = _row_tile(mrows, 256)
    return pl.pallas_call(
        _final_norm_kernel,
        grid=(mrows // tm,),
        in_specs=[pl.BlockSpec((tm, d), lambda m: (m, 0)), pl.BlockSpec((1, d), lambda m: (0, 0))],
        out_specs=pl.BlockSpec((tm, d), lambda m: (m, 0)),
        out_shape=jax.ShapeDtypeStruct((mrows, d), F32),
        compiler_params=_params("arbitrary"),
        name="final_norm",
    )(x, g.reshape(1, d))


def _bf16_tiles(m, *pairs):
    todo = [(w_ref, keep_ref) for w_ref, keep_ref in pairs if w_ref.dtype != BF16]
    if todo:
        @pl.when(m == 0)
        def _():
            for w_ref, keep_ref in todo:
                keep_ref[...] = w_ref[...].astype(BF16)

    return [w_ref if w_ref.dtype == BF16 else keep_ref for w_ref, keep_ref in pairs]


def _weight_spec(w, layer, k, tn, index, **mode):
    if w.dtype == BF16:
        return pl.BlockSpec((k, tn), lambda j, m: (0, index(j)), **mode)
    return pl.BlockSpec((None, k, tn), lambda j, m: (layer, 0, index(j)), **mode)


def _mm_kernel(*refs, has_bias, has_res, tm, rs, per_row):
    it = iter(refs)
    x_ref, w_ref = next(it), next(it)
    b_ref = next(it) if has_bias else None
    res_ref, g_ref = (next(it), next(it)) if has_res else (None, None)
    o_ref, keep_ref = next(it), next(it, None)
    m = pl.program_id(1)
    (rhs_ref,) = _bf16_tiles(m, (w_ref, keep_ref))
    acc = jnp.dot(x_ref[...], rhs_ref[...], preferred_element_type=F32)
    if has_bias:
        acc = acc + b_ref[...]
    if has_res:
        acc = res_ref[...] + _mod_value(g_ref, m, tm, rs, per_row) * acc
    if len(o_ref.shape) == 3:
        for c in range(o_ref.shape[0]):
            o_ref[c] = acc[:, c * HEAD_DIM:(c + 1) * HEAD_DIM].astype(o_ref.dtype)
    else:
        o_ref[...] = acc.astype(o_ref.dtype)


def _mm(x, w, layer, *, out_dtype, bias=None, res=None, gate=None, tm_pref=1024, tn_pref=512,
        head_major=False, single_buffer_w=False, emit_bf16=False):
    mrows, k = x.shape
    n = w.shape[-1]
    tm = _row_tile(mrows, tm_pref)
    tn = _col_tile(n, tn_pref)
    w_mode = dict(pipeline_mode=pl.Buffered(1)) if single_buffer_w else {}
    in_specs = [pl.BlockSpec((tm, k), lambda j, m: (m, 0)),
                _weight_spec(w, layer, k, tn, lambda j: j, **w_mode)]
    args = [x, w]
    if bias is not None:
        in_specs.append(pl.BlockSpec((None, 1, tn), lambda j, m: (layer, 0, j)))
        args.append(bias.reshape(bias.shape[0], 1, n))
    rs, per_row = 1, True
    if res is not None:
        mod, gl, gate_chunk = gate
        rs, per_row = mod.rs, mod.per_row
        goff = gate_chunk * (n // tn)
        in_specs.append(pl.BlockSpec((tm, tn), lambda j, m: (m, j)))
        if per_row:
            in_specs.append(pl.BlockSpec((None, tm, tn), lambda j, m: (gl, m, goff + j)))
        else:
            in_specs.append(pl.BlockSpec((None, ADA_ROWS, tn), lambda j, m: (gl, 0, goff + j)))
        args += [res, mod.arr]
    if head_major:
        out_spec = pl.BlockSpec((tn // HEAD_DIM, tm, HEAD_DIM), lambda j, m: (j, m, 0))
        out_shape = jax.ShapeDtypeStruct((n // HEAD_DIM, mrows, HEAD_DIM), out_dtype)
    else:
        out_spec = pl.BlockSpec((tm, tn), lambda j, m: (m, j))
        out_shape = jax.ShapeDtypeStruct((mrows, n), out_dtype)
    scratch = []
    if emit_bf16:
        out_spec = (out_spec, pl.BlockSpec((k, tn), lambda j, m: (0, j)))
        out_shape = (out_shape, jax.ShapeDtypeStruct((k, n), BF16))
    elif w.dtype != BF16:
        scratch = [pltpu.VMEM((k, tn), BF16)]
    return pl.pallas_call(
        functools.partial(_mm_kernel, has_bias=bias is not None, has_res=res is not None,
                          tm=tm, rs=rs, per_row=per_row),
        grid=(n // tn, mrows // tm),
        in_specs=in_specs,
        out_specs=out_spec,
        out_shape=out_shape,
        scratch_shapes=scratch,
        compiler_params=_params("arbitrary", "arbitrary"),
        name="matmul",
    )(*args)


def _causal_taps(u, dw_ref, row, p0, p1):
    u1 = jnp.where(row == 0, p1, pltpu.roll(u, 1, 0))
    u2 = jnp.where(row == 0, p0, jnp.where(row == 1, p1, pltpu.roll(u, 2, 0)))
    return dw_ref[0:1, :] * u2 + dw_ref[1:2, :] * u1 + dw_ref[2:3, :] * u


def _pair_kernel(*refs, mode, rs):
    it = iter(refs)
    x_ref, wa_ref, wb_ref = next(it), next(it), next(it)
    if mode == "glu":
        ba_ref, bb_ref = next(it), next(it)
    else:
        dwa_ref, dwb_ref = next(it), next(it)
        c0a_ref, c0b_ref, c1a_ref, c1b_ref = next(it), next(it), next(it), next(it)
    o_ref = next(it)
    if mode == "ffn":
        ta_ref, tb_ref = next(it), next(it)
    keep_a_ref, keep_b_ref = next(it, None), next(it, None)
    m = pl.program_id(1)
    rhs_a_ref, rhs_b_ref = _bf16_tiles(m, (wa_ref, keep_a_ref), (wb_ref, keep_b_ref))
    x = x_ref[...]
    ua = jnp.dot(x, rhs_a_ref[...], preferred_element_type=F32)
    ub = jnp.dot(x, rhs_b_ref[...], preferred_element_type=F32)

    if mode == "glu":
        o_ref[...] = ((ua + ba_ref[...]) * jax.nn.sigmoid(ub + bb_ref[...])).astype(o_ref.dtype)
        return

    assert rs & (rs - 1) == 0
    row = lax.broadcasted_iota(jnp.int32, ua.shape, 0) & (rs - 1)
    gate = _causal_taps(ua, dwa_ref, row, c0a_ref[...], c1a_ref[...])
    val = _causal_taps(ub, dwb_ref, row, c0b_ref[...], c1b_ref[...])
    ta_ref[...] = ua
    tb_ref[...] = ub
    o_ref[...] = ((gate * jax.nn.sigmoid(gate)) * val).astype(o_ref.dtype)


def _pair_weights(w, layer, k, half, tn):
    if isinstance(w, tuple):
        return [_weight_spec(w[0], layer, k, tn, lambda j: j), _weight_spec(w[1], layer, k, tn, lambda j: j)], list(w)
    hoff = half // tn
    return [_weight_spec(w, layer, k, tn, lambda j: j), _weight_spec(w, layer, k, tn, lambda j: hoff + j)], [w, w]


def _pair_mm(x, w, layer, *, mode, rs, bias=None, dw=None, ctx=None, tm_pref=1024, tn_pref=256, emit_bf16=False):
    mrows, k = x.shape
    rounded = isinstance(w, tuple)
    half = w[0].shape[1] if rounded else w.shape[2] // 2
    n2 = 2 * half
    tm = _row_tile(mrows, tm_pref)
    tn = _col_tile(half, tn_pref)
    hoff = half // tn
    assert mode == "glu" or tm % rs == 0
    w_specs, w_args = _pair_weights(w, layer, k, half, tn)
    in_specs = [pl.BlockSpec((tm, k), lambda j, m: (m, 0))] + w_specs
    args = [x] + w_args
    scratch = [] if rounded or emit_bf16 else [pltpu.VMEM((k, tn), BF16), pltpu.VMEM((k, tn), BF16)]
    if mode == "glu":
        b3 = bias.reshape(bias.shape[0], 1, n2)
        in_specs += [pl.BlockSpec((None, 1, tn), lambda j, m: (layer, 0, j)),
                     pl.BlockSpec((None, 1, tn), lambda j, m: (layer, 0, hoff + j))]
        args += [b3, b3]
        out_shape = jax.ShapeDtypeStruct((mrows, half), F32)
        out_specs = pl.BlockSpec((tm, tn), lambda j, m: (m, j))
        if emit_bf16:
            out_shape = (out_shape,) + (jax.ShapeDtypeStruct((k, half), BF16),) * 2
            out_specs = (out_specs,) + (pl.BlockSpec((k, tn), lambda j, m: (0, j)),) * 2
    else:
        taps = dw.shape[1]
        in_specs += [pl.BlockSpec((None, taps, tn), lambda j, m: (layer, 0, j)),
                     pl.BlockSpec((None, taps, tn), lambda j, m: (layer, 0, hoff + j))]
        args += [dw, dw]
        c0, c1 = ctx
        lo = pl.BlockSpec((tm, tn), lambda j, m: (m, j))
        hi = pl.BlockSpec((tm, tn), lambda j, m: (m, hoff + j))
        in_specs += [lo, hi, lo, hi]
        args += [c0, c0, c1, c1]
        out_shape = (jax.ShapeDtypeStruct((mrows, half), BF16),) + (jax.ShapeDtypeStruct((mrows, half), F32),) * 2
        out_specs = (lo, lo, lo)
    return pl.pallas_call(
        functools.partial(_pair_kernel, mode=mode, rs=rs),
        grid=(half // tn, mrows // tm),
        in_specs=in_specs,
        out_specs=out_specs,
        out_shape=out_shape,
        scratch_shapes=scratch,
        compiler_params=_params("arbitrary", "arbitrary"),
        name="pair_" + mode,
    )(*args)


_FFN_CHUNK_ROWS = 256


def _ffn_up_kernel(x_ref, wa_ref, wb_ref, dwa_ref, dwb_ref, o_ref, ta_ref, tb_ref,
                   wabf_ref, wbbf_ref, ua_ref, ub_ref, *, tm, rs):
    m = pl.program_id(1)
    rc = min(_FFN_CHUNK_ROWS, tm)

    _bf16_tiles(m, (wa_ref, wabf_ref), (wb_ref, wbbf_ref))

    @pl.when((m * tm) % rs == 0)
    def _():
        ua_ref[0:8, :] = jnp.zeros((8, ua_ref.shape[1]), F32)
        ub_ref[0:8, :] = jnp.zeros((8, ub_ref.shape[1]), F32)

    x = x_ref[...]
    ua_ref[8:tm + 8, :] = jnp.dot(x, wabf_ref[...], preferred_element_type=F32)
    ub_ref[8:tm + 8, :] = jnp.dot(x, wbbf_ref[...], preferred_element_type=F32)

    def chunk(c, carry):
        r0 = pl.multiple_of(c * rc, rc)

        def taps(u_ref, dw_ref):
            u = u_ref[pl.ds(r0, rc + 8), :]
            y = dw_ref[0:1, :] * pltpu.roll(u, 2, 0) + dw_ref[1:2, :] * pltpu.roll(u, 1, 0) + dw_ref[2:3, :] * u
            return y[8:, :]

        gate = taps(ua_ref, dwa_ref)
        val = taps(ub_ref, dwb_ref)
        o_ref[pl.ds(r0, rc), :] = ((gate * jax.nn.sigmoid(gate)) * val).astype(o_ref.dtype)
        return carry

    lax.fori_loop(0, tm // rc, chunk, 0)

    @pl.when(((m + 1) * tm) % rs == 0)
    def _():
        ta_ref[...] = ua_ref[tm:tm + 8, :]
        tb_ref[...] = ub_ref[tm:tm + 8, :]

    ua_ref[0:8, :] = ua_ref[tm:tm + 8, :]
    ub_ref[0:8, :] = ub_ref[tm:tm + 8, :]


def _ffn_up(x, w, layer, dw, *, rs, tm_pref=1024, tn_pref=256):
    mrows, k = x.shape
    half = w.shape[2] // 2
    taps = dw.shape[1]
    tm = _row_tile(mrows, tm_pref)
    tn = _col_tile(half, tn_pref)
    hoff = half // tn
    assert rs % tm == 0 and taps == 3
    tail_shape = jax.ShapeDtypeStruct((mrows // rs, 8, half), F32)
    tail_spec = pl.BlockSpec((None, 8, tn), lambda j, m: ((m * tm) // rs, 0, j))
    wbf_shape = jax.ShapeDtypeStruct((k, half), BF16)
    wbf_spec = pl.BlockSpec((k, tn), lambda j, m: (0, j))
    return pl.pallas_call(
        functools.partial(_ffn_up_kernel, tm=tm, rs=rs),
        grid=(half // tn, mrows // tm),
        in_specs=[pl.BlockSpec((tm, k), lambda j, m: (m, 0)),
                  pl.BlockSpec((None, k, tn), lambda j, m: (layer, 0, j)),
                  pl.BlockSpec((None, k, tn), lambda j, m: (layer, 0, hoff + j)),
                  pl.BlockSpec((None, taps, tn), lambda j, m: (layer, 0, j)),
                  pl.BlockSpec((None, taps, tn), lambda j, m: (layer, 0, hoff + j))],
        out_specs=(pl.BlockSpec((tm, tn), lambda j, m: (m, j)), tail_spec, tail_spec, wbf_spec, wbf_spec),
        out_shape=(jax.ShapeDtypeStruct((mrows, half), BF16), tail_shape, tail_shape, wbf_shape, wbf_shape),
        scratch_shapes=[pltpu.VMEM((tm + 8, tn), F32)] * 2,
        compiler_params=_params("arbitrary", "arbitrary"),
        name="ffn_up",
    )(x, w, w, dw, dw)


def _band_bias(n_keys, n_steps):
    qi = lax.broadcasted_iota(jnp.int32, (Q_BLOCK, n_keys), 0)
    ki = lax.broadcasted_iota(jnp.int32, (Q_BLOCK, n_keys), 1)
    dist = qi + (n_keys - Q_BLOCK) - ki
    return jnp.where((dist >= 0) & (dist <= n_steps), 0.0, -jnp.inf).astype(F32)


def _attn_prompt_kernel(*refs, seq):
    qkv_refs, out_ref, scr = refs[:3 * N_GROUPS], refs[3 * N_GROUPS], refs[3 * N_GROUPS + 1:]
    scale = HEAD_DIM ** -0.5
    nt = (((1,), (1,)), ((), ()))
    for g, (window, dil) in enumerate(DILATED_GROUPS):
        q_ref, k_ref, v_ref = qkv_refs[3 * g:3 * g + 3]
        o_scr, l_scr = scr[2 * g], scr[2 * g + 1]
        n_steps = window // dil
        bias_first = _band_bias(Q_BLOCK, n_steps)
        bias_full = _band_bias(2 * Q_BLOCK, n_steps)
        for r in range(dil):
            for blk in range(seq // dil // Q_BLOCK):
                start = r + dil * Q_BLOCK * blk

                def rows(first, count):
                    return pl.ds(first, count, stride=dil) if dil > 1 else pl.ds(first, count)

                q = q_ref[rows(start, Q_BLOCK), :].astype(BF16)
                if blk == 0:
                    keys, bias = rows(start, Q_BLOCK), bias_first
                else:
                    keys, bias = rows(start - dil * Q_BLOCK, 2 * Q_BLOCK), bias_full
                k = k_ref[keys, :].astype(BF16)
                v = v_ref[keys, :].astype(BF16)
                s = lax.dot_general(q, k, nt, preferred_element_type=F32) * scale + bias
                mx = jnp.max(s, axis=-1, keepdims=True)
                p = jnp.exp(s - mx)
                den = jnp.sum(p, axis=-1, keepdims=True)
                o_scr[rows(start, Q_BLOCK), :] = jnp.dot(p.astype(BF16), v, preferred_element_type=F32) / den
                l_scr[rows(start, Q_BLOCK), :] = jnp.broadcast_to(mx + jnp.log(den), (Q_BLOCK, HEAD_DIM))
    l1, l2, l3 = scr[1][...], scr[3][...], scr[5][...]
    mx = jnp.maximum(jnp.maximum(l1, l2), l3)
    e1, e2, e3 = jnp.exp(l1 - mx), jnp.exp(l2 - mx), jnp.exp(l3 - mx)
    num = e1 * scr[0][...] + e2 * scr[2][...] + e3 * scr[4][...]
    out_ref[...] = (num / (e1 + e2 + e3)).astype(out_ref.dtype)


def _attn_prompt(qkv_t, nseq, seq):
    heads = qkv_t.shape[0] // (3 * N_GROUPS)
    for window, dil in DILATED_GROUPS:
        assert seq % (dil * Q_BLOCK) == 0 and window // dil <= Q_BLOCK
    in_specs = [pl.BlockSpec((None, seq, HEAD_DIM), lambda b, h, c=c: (c * heads + h, b, 0))
                for c in range(3 * N_GROUPS)]
    return pl.pallas_call(
        functools.partial(_attn_prompt_kernel, seq=seq),
        grid=(nseq, heads),
        in_specs=in_specs,
        out_specs=pl.BlockSpec((seq, HEAD_DIM), lambda b, h: (b, h)),
        out_shape=jax.ShapeDtypeStruct((nseq * seq, heads * HEAD_DIM), BF16),
        scratch_shapes=[pltpu.VMEM((seq, HEAD_DIM), F32)] * (2 * N_GROUPS),
        compiler_params=_params("arbitrary", "arbitrary"),
        name="attn_prompt",
    )(*([qkv_t] * (3 * N_GROUPS)))


def _attn_sample_kernel(*refs, heads, steps):
    qkv_refs, cache_refs, o_ref = refs[:3 * N_GROUPS], refs[3 * N_GROUPS:4 * N_GROUPS], refs[4 * N_GROUPS]
    assert heads & (heads - 1) == 0
    t = pl.program_id(1)
    scale = HEAD_DIM ** -0.5
    nt = (((1,), (1,)), ((), ()))
    log_heads = jnp.int32(heads.bit_length() - 1)

    def columns(n_pos):
        col = lax.broadcasted_iota(jnp.int32, (heads, n_pos * heads), 1)
        row = lax.broadcasted_iota(jnp.int32, (heads, n_pos * heads), 0)
        return (col & (heads - 1)) == row, lax.shift_right_logical(col, log_heads)

    own_c, pos_c = columns(Q_BLOCK)
    own_n, pos_n = columns(steps)
    scores, values = [], []
    for g, (_, dil) in enumerate(DILATED_GROUPS):
        q_ref, kn_ref, vn_ref = qkv_refs[3 * g:3 * g + 3]
        c_ref = cache_refs[g]
        q = q_ref[...].astype(BF16)
        kc = c_ref[:, 0:heads, :].reshape(Q_BLOCK * heads, HEAD_DIM).astype(BF16)
        kn = kn_ref[...].reshape(steps * heads, HEAD_DIM).astype(BF16)
        sc = lax.dot_general(q, kc, nt, preferred_element_type=F32) * scale
        sn = lax.dot_general(q, kn, nt, preferred_element_type=F32) * scale
        sc = jnp.where(own_c & (pos_c >= t // dil), sc, -jnp.inf)
        back = t - pos_n
        sn = jnp.where(own_n & (back >= 0) & ((back & (dil - 1)) == 0), sn, -jnp.inf)
        scores += [sc, sn]
        values += [c_ref[:, heads:2 * heads, :].reshape(Q_BLOCK * heads, HEAD_DIM).astype(BF16),
                   vn_ref[...].reshape(steps * heads, HEAD_DIM).astype(BF16)]
    mx = functools.reduce(jnp.maximum, [jnp.max(s, axis=-1, keepdims=True) for s in scores])
    den = jnp.zeros((heads, 1), F32)
    acc = jnp.zeros((heads, HEAD_DIM), F32)
    for s, v in zip(scores, values):
        p = jnp.exp(s - mx)
        den = den + jnp.sum(p, axis=-1, keepdims=True)
        acc = acc + jnp.dot(p.astype(BF16), v, preferred_element_type=F32)
    o_ref[...] = acc / den


def _attn_sample(qkv, caches, layer, nseq, steps):
    aw = qkv.shape[1] // (3 * N_GROUPS)
    heads = aw // HEAD_DIM
    qkv4 = qkv.reshape(nseq, steps, 3 * N_GROUPS * heads, HEAD_DIM)
    in_specs, args = [], []
    for g in range(N_GROUPS):
        in_specs += [pl.BlockSpec((None, None, heads, HEAD_DIM), lambda b, t, g=g: (b, t, 3 * g, 0)),
                     pl.BlockSpec((None, steps, heads, HEAD_DIM), lambda b, t, g=g: (b, 0, 3 * g + 1, 0)),
                     pl.BlockSpec((None, steps, heads, HEAD_DIM), lambda b, t, g=g: (b, 0, 3 * g + 2, 0))]
        args += [qkv4] * 3
    for (window, dil), cache in zip(DILATED_GROUPS, caches):
        assert cache.shape[2] == window and window // dil == Q_BLOCK
        args.append(cache.reshape(cache.shape[0] * nseq, Q_BLOCK, dil, 2 * heads, HEAD_DIM))
        in_specs.append(pl.BlockSpec((None, Q_BLOCK, None, 2 * heads, HEAD_DIM),
                                     lambda b, t, dil=dil: (layer * nseq + b, 0, t % dil, 0, 0)))
    out = pl.pallas_call(
        functools.partial(_attn_sample_kernel, heads=heads, steps=steps),
        grid=(nseq, steps),
        in_specs=in_specs,
        out_specs=pl.BlockSpec((None, None, heads, HEAD_DIM), lambda b, t: (b, t, 0, 0)),
        out_shape=jax.ShapeDtypeStruct((nseq, steps, heads, HEAD_DIM), F32),
        compiler_params=_params("arbitrary", "arbitrary"),
        name="attn_sample",
    )(*args)
    return out.reshape(nseq * steps, aw)


_SLIDE_BLOCK_BYTES = 8 * 1024 * 1024


def _slide_kernel(main_ref, next_ref, new_ref, o_ref, *, groups):
    i = pl.program_id(1)
    last = pl.num_programs(1) - 1
    if groups > 1:
        o_ref[0:groups - 1] = main_ref[1:groups]

    @pl.when(i < last)
    def _():
        o_ref[groups - 1] = next_ref[0]

    @pl.when(i == last)
    def _():
        o_ref[groups - 1] = new_ref[...]


def _slide_cache(cache, layer, new_rows):
    layers, nseq, window = cache.shape[:3]
    steps = new_rows.shape[1]
    rows = steps * cache.shape[3] * cache.shape[4]
    assert window % steps == 0 and rows % 8 == 0
    ngroups = window // steps
    groups = _row_tile(ngroups, max(1, _SLIDE_BLOCK_BYTES // (rows * HEAD_DIM * 4)))
    view = cache.reshape(layers * nseq, ngroups, rows, HEAD_DIM)
    blk = (None, groups, rows, HEAD_DIM)
    out = pl.pallas_call(
        functools.partial(_slide_kernel, groups=groups),
        grid=(nseq, ngroups // groups),
        in_specs=[pl.BlockSpec(blk, lambda b, i: (layer * nseq + b, i, 0, 0)),
                  pl.BlockSpec((None, 1, rows, HEAD_DIM),
                               lambda b, i: (layer * nseq + b, jnp.minimum((i + 1) * groups, ngroups - 1), 0, 0)),
                  pl.BlockSpec((None, rows, HEAD_DIM), lambda b, i: (b, 0, 0))],
        out_specs=pl.BlockSpec(blk, lambda b, i: (b, i, 0, 0)),
        out_shape=jax.ShapeDtypeStruct((nseq, ngroups, rows, HEAD_DIM), F32),
        compiler_params=_params("arbitrary", "arbitrary"),
        name="cache_slide",
    )(view, view, new_rows.reshape(nseq, rows, HEAD_DIM))
    return out.reshape((nseq,) + cache.shape[2:])


_CONV_ROWS = 128
_CONV_COLS = 128


def _dwln_kernel(u_ref, ctx_ref, w_ref, bdw_ref, g_ref, b_ref, o_ref, ext_ref, y_ref, *, tm, rs, taps, zero_start):
    m = pl.program_id(0)
    if zero_start:
        at_start = (m * tm) % rs == 0

        @pl.when(at_start)
        def _():
            ext_ref[0:CONV_CTX_ROWS, :] = jnp.zeros((CONV_CTX_ROWS, ext_ref.shape[1]), F32)

        @pl.when(jnp.logical_not(at_start))
        def _():
            ext_ref[0:CONV_CTX_ROWS, :] = ctx_ref[...]
    else:
        ext_ref[0:CONV_CTX_ROWS, :] = ctx_ref[...]
    ext_ref[CONV_CTX_ROWS:CONV_CTX_ROWS + tm, :] = u_ref[...]
    ext_ref[CONV_CTX_ROWS + tm:CONV_CTX_ROWS + tm + 8, :] = jnp.zeros((8, ext_ref.shape[1]), F32)

    d = u_ref.shape[1]
    rc = min(_CONV_ROWS, tm)
    cw = min(_CONV_COLS, d)
    first = CONV_CTX_ROWS - (taps - 1)
    for r0 in range(0, tm, rc):
        for c0 in range(0, d, cw):
            acc = jnp.zeros((rc, cw), F32)
            for s in range(8):
                part = None
                for k in range(taps):
                    if (first + k) % 8 != s:
                        continue
                    a = r0 + first + k - s
                    term = w_ref[k:k + 1, c0:c0 + cw] * ext_ref[a:a + rc + 8, c0:c0 + cw]
                    part = term if part is None else part + term
                if part is not None:
                    acc = acc + part[s:s + rc, :]
            y_ref[r0:r0 + rc, c0:c0 + cw] = acc + bdw_ref[:, c0:c0 + cw]

    y = y_ref[...]
    mu = jnp.mean(y, axis=-1, keepdims=True)
    yc = y - mu
    var = jnp.mean(yc * yc, axis=-1, keepdims=True)
    z = yc * lax.rsqrt(var + LN_EPS) * g_ref[...] + b_ref[...]
    o_ref[...] = (z * jax.nn.sigmoid(z)).astype(o_ref.dtype)


def _dwln(u, ctx, layer, w_dw, b_dw, ln_g, ln_b, *, rs, tm):
    mrows, d = u.shape
    taps = w_dw.shape[1]
    assert taps - 1 <= CONV_CTX_ROWS and tm % 8 == 0
    wpad = jnp.pad(w_dw, ((0, 0), (0, CONV_CTX_ROWS - taps), (0, 0)))
    zero_start = ctx is None
    if zero_start:
        assert tm % CONV_CTX_ROWS == 0
        per = tm // CONV_CTX_ROWS
        ctx_arr = u
        ctx_spec = pl.BlockSpec((CONV_CTX_ROWS, d), lambda m: (jnp.maximum(m * per - 1, 0), 0))
    else:
        assert tm == rs
        ctx_arr = ctx
        ctx_spec = pl.BlockSpec((None, CONV_CTX_ROWS, d), lambda m: (m, 0, 0))
    vec = lambda a: a.reshape(a.shape[0], 1, d)
    vec_spec = pl.BlockSpec((None, 1, d), lambda m: (layer, 0, 0))
    return pl.pallas_call(
        functools.partial(_dwln_kernel, tm=tm, rs=rs, taps=taps, zero_start=zero_start),
        grid=(mrows // tm,),
        in_specs=[pl.BlockSpec((tm, d), lambda m: (m, 0)), ctx_spec,
                  pl.BlockSpec((None, CONV_CTX_ROWS, d), lambda m: (layer, 0, 0)),
                  vec_spec, vec_spec, vec_spec],
        out_specs=pl.BlockSpec((tm, d), lambda m: (m, 0)),
        out_shape=jax.ShapeDtypeStruct((mrows, d), BF16),
        scratch_shapes=[pltpu.VMEM((CONV_CTX_ROWS + tm + 8, d), F32), pltpu.VMEM((tm, d), F32)],
        compiler_params=_params("arbitrary"),
        name="dwconv_ln",
    )(u, ctx_arr, wpad, vec(b_dw), vec(ln_g), vec(ln_b))


def _trunk(x, mod, nseq, rs, params, kv_caches, conv_states, ffn_states, rounded):
    (w_qkv, w_o, w_pw1, b_pw1, w_dw, b_dw, ln_g, ln_b, w_pw2, b_pw2,
     w_up, w_ffn_dw, w_down, final_norm_g) = params
    depth = w_up.shape[0]
    d = x.shape[1]
    prompt = kv_caches is None
    new_kv = [[] for _ in range(N_GROUPS)]
    new_conv, new_ffn = [], []

    def gated(name, xin, w, layer, **kw):
        if prompt:
            out, rounded[name, layer] = _mm(xin, w, layer, out_dtype=F32, emit_bf16=True, **kw)
            return out
        return _mm(xin, rounded[name, layer], layer, out_dtype=F32, **kw)

    for i in range(depth):
        j = i // 2
        h = _modulate(x, mod, i, 0)
        if i % 2 == 0:
            heads = w_qkv.shape[2] // (3 * N_GROUPS * HEAD_DIM)
            if prompt:
                qkv_t, rounded["qkv", j] = _mm(h, w_qkv, j, out_dtype=F32, head_major=True, emit_bf16=True)
                mixed = _attn_prompt(qkv_t, nseq, rs)
                qkv5 = qkv_t.reshape(N_GROUPS, 3, heads, nseq, rs, HEAD_DIM)
                for g, (window, _) in enumerate(DILATED_GROUPS):
                    keep = min(window, rs)
                    new_kv[g].append(jnp.transpose(qkv5[g, 1:, :, :, rs - keep:], (2, 3, 0, 1, 4)))
            else:
                qkv = _mm(h, rounded["qkv", j], j, out_dtype=F32)
                mixed = _attn_sample(qkv, kv_caches, j, nseq, rs).astype(BF16)
                qkv5 = qkv.reshape(nseq, rs, N_GROUPS, 3, heads, HEAD_DIM)
                for g in range(N_GROUPS):
                    new_kv[g].append(_slide_cache(kv_caches[g], j, qkv5[:, :, g, 1:]))
            x = gated("o", mixed, w_o, j, res=x, gate=(mod, i, 2))
        else:
            if prompt:
                u, *rounded["pw1", j] = _pair_mm(h, w_pw1, j, mode="glu", rs=rs, bias=b_pw1, emit_bf16=True)
            else:
                u = _pair_mm(h, tuple(rounded["pw1", j]), j, mode="glu", rs=rs, bias=b_pw1)
            taps = w_dw.shape[1]
            if prompt:
                y = _dwln(u, None, j, w_dw, b_dw, ln_g, ln_b, rs=rs, tm=128)
                new_conv.append(u.reshape(nseq, rs, d)[:, rs - (taps - 1):])
            else:
                state = conv_states[j]
                ctx = jnp.pad(state, ((0, 0), (CONV_CTX_ROWS - (taps - 1), 0), (0, 0)))
                y = _dwln(u, ctx, j, w_dw, b_dw, ln_g, ln_b, rs=rs, tm=rs)
                u_all = jnp.concatenate([state, u.reshape(nseq, rs, d)], axis=1)
                new_conv.append(u_all[:, -(taps - 1):])
            x = gated("pw2", y, w_pw2, j, bias=b_pw2, res=x, gate=(mod, i, 2))
        h = _modulate(x, mod, i, 3)
        if prompt:
            act, ta, tb, *rounded["up", i] = _ffn_up(h, w_up, i, w_ffn_dw, rs=rs)
            new_ffn.append(jnp.concatenate([ta[:, 6:8], tb[:, 6:8]], axis=-1))
        else:
            state = ffn_states[i]
            c0 = jnp.repeat(state[:, 0], rs, axis=0)
            c1 = jnp.repeat(state[:, 1], rs, axis=0)
            act, ta, tb = _pair_mm(h, tuple(rounded["up", i]), i, mode="ffn", rs=rs, dw=w_ffn_dw, ctx=(c0, c1))
            u_ffn = jnp.concatenate([ta, tb], axis=-1).reshape(nseq, rs, -1)
            new_ffn.append(jnp.concatenate([state, u_ffn], axis=1)[:, -state.shape[1]:])
        x = _mm(act, w_down, i, out_dtype=F32, res=x, gate=(mod, i, 5), tm_pref=256, tn_pref=512,
                single_buffer_w=True)
    y = _final_norm(x, final_norm_g)
    return y, [jnp.stack(l) for l in new_kv], jnp.stack(new_conv), jnp.stack(new_ffn)


def kernel(x_prompt, x_sample, c_prompt, c_sample, cache_kv_g1, cache_kv_g2, cache_kv_g3, state_conv, state_ffn_conv, w_ada, b_ada, w_qkv, w_o, w_pw1, b_pw1, w_dw, b_dw, ln_g, ln_b, w_pw2, b_pw2, w_up, w_ffn_dw, w_down, final_norm_g):
    nb, seq, d = x_prompt.shape
    ndb, steps, _ = x_sample.shape
    assert nb + ndb <= ADA_ROWS
    params = (w_qkv, w_o, w_pw1, b_pw1, w_dw, b_dw, ln_g, ln_b, w_pw2, b_pw2,
              w_up, w_ffn_dw, w_down, final_norm_g)

    c_rows = jnp.concatenate([c_prompt, c_sample, jnp.zeros((ADA_ROWS - nb - ndb, d), F32)], axis=0)
    ada = _adaln(c_rows, w_ada, b_ada)
    mod_p = _Mod(ada, False, seq)
    mod_s = _Mod(jnp.repeat(ada[:, nb:nb + ndb], steps, axis=1), True, steps)

    rounded = {}
    y_p, kv_p, conv_p, ffn_p = _trunk(x_prompt.reshape(nb * seq, d), mod_p, nb, seq, params, None, None, None,
                                      rounded)
    y_s, kv_s, conv_s, ffn_s = _trunk(x_sample.reshape(ndb * steps, d), mod_s, ndb, steps, params,
                                      (cache_kv_g1, cache_kv_g2, cache_kv_g3), state_conv, state_ffn_conv, rounded)
    return (y_p.reshape(nb, seq, d), y_s.reshape(ndb, steps, d), kv_p[0], kv_p[1], kv_p[2], conv_p, ffn_p,
            kv_s[0], kv_s[1], kv_s[2], conv_s, ffn_s)
```

```python
import functools

import jax
import jax.numpy as jnp
from jax import lax
from jax.experimental import pallas as pl
from jax.experimental.pallas import tpu as pltpu

F32 = jnp.float32
BF16 = jnp.bfloat16

HEAD_DIM = 128
Q_BLOCK = 128
DILATED_GROUPS = ((128, 1), (512, 4), (2048, 16))
N_GROUPS = len(DILATED_GROUPS)
RMS_EPS = 1e-6
LN_EPS = 1e-5
ADA_ROWS = 16
CONV_CTX_ROWS = 32
V7X_VMEM_LIMIT_BYTES = 56 * 1024 * 1024


def _params(*sem):
    return pltpu.CompilerParams(dimension_semantics=sem, vmem_limit_bytes=V7X_VMEM_LIMIT_BYTES)


def _col_tile(n, pref):
    if n <= pref:
        return n
    t = pref - pref % 128
    while t >= 128:
        if n % t == 0:
            return t
        t -= 128
    raise ValueError(f"no lane-aligned tile for {n}")


def _row_tile(m, pref):
    t = 1
    while t * 2 <= pref and m % (t * 2) == 0:
        t *= 2
    return t


class _Mod:
    def __init__(self, arr, per_row, rows_per_seq):
        self.arr, self.per_row, self.rs = arr, per_row, rows_per_seq


def _mod_value(ref, m, tm, rs, per_row):
    if per_row:
        return ref[...]
    return ref[pl.ds((m * tm) // rs, 1), :]


def _adaln_kernel(c_ref, w_ref, b_ref, o_ref):
    c = c_ref[...]
    s = (c * jax.nn.sigmoid(c)).astype(BF16)
    o_ref[...] = jnp.dot(s, w_ref[...].astype(BF16), preferred_element_type=F32) + b_ref[...]


def _adaln(c_rows, w_ada, b_ada):
    depth, d, n = w_ada.shape
    tn = _col_tile(n, 1024)
    return pl.pallas_call(
        _adaln_kernel,
        grid=(depth, n // tn),
        in_specs=[pl.BlockSpec((ADA_ROWS, d), lambda i, j: (0, 0)),
                  pl.BlockSpec((None, d, tn), lambda i, j: (i, 0, j)),
                  pl.BlockSpec((None, 1, tn), lambda i, j: (i, 0, j))],
        out_specs=pl.BlockSpec((None, ADA_ROWS, tn), lambda i, j: (i, 0, j)),
        out_shape=jax.ShapeDtypeStruct((depth, ADA_ROWS, n), F32),
        compiler_params=_params("arbitrary", "arbitrary"),
        name="adaln",
    )(c_rows, w_ada, b_ada.reshape(depth, 1, n))


def _modulate_kernel(x_ref, sh_ref, sc_ref, o_ref, *, tm, rs, per_row):
    m = pl.program_id(0)
    x = x_ref[...]
    r = lax.rsqrt(jnp.mean(x * x, axis=-1, keepdims=True) + RMS_EPS)
    sh = _mod_value(sh_ref, m, tm, rs, per_row)
    sc = _mod_value(sc_ref, m, tm, rs, per_row)
    o_ref[...] = ((x * r) * (1.0 + sc) + sh).astype(o_ref.dtype)


def _modulate(x, mod, layer, shift_chunk):
    mrows, d = x.shape
    tm = _row_tile(mrows, 512)
    if mod.per_row:
        spec = lambda c: pl.BlockSpec((None, tm, d), lambda m: (layer, m, c))
    else:
        spec = lambda c: pl.BlockSpec((None, ADA_ROWS, d), lambda m: (layer, 0, c))
    return pl.pallas_call(
        functools.partial(_modulate_kernel, tm=tm, rs=mod.rs, per_row=mod.per_row),
        grid=(mrows // tm,),
        in_specs=[pl.BlockSpec((tm, d), lambda m: (m, 0)), spec(shift_chunk), spec(shift_chunk + 1)],
        out_specs=pl.BlockSpec((tm, d), lambda m: (m, 0)),
        out_shape=jax.ShapeDtypeStruct((mrows, d), BF16),
        compiler_params=_params("arbitrary"),
        name="modulate",
    )(x, mod.arr, mod.arr)


def _final_norm_kernel(x_ref, g_ref, o_ref):
    x = x_ref[...]
    r = lax.rsqrt(jnp.mean(x * x, axis=-1, keepdims=True) + RMS_EPS)
    o_ref[...] = (x * r) * g_ref[...]


def _final_norm(x, g):
    mrows, d = x.shape
    tm = _row_tile(mrows, 512)
    return pl.pallas_call(
        _final_norm_kernel,
        grid=(mrows // tm,),
        in_specs=[pl.BlockSpec((tm, d), lambda m: (m, 0)), pl.BlockSpec((1, d), lambda m: (0, 0))],
        out_specs=pl.BlockSpec((tm, d), lambda m: (m, 0)),
        out_shape=jax.ShapeDtypeStruct((mrows, d), F32),
        compiler_params=_params("arbitrary"),
        name="final_norm",
    )(x, g.reshape(1, d))


def _bf16_tiles(m, *pairs):
    todo = [(w_ref, keep_ref) for w_ref, keep_ref in pairs if w_ref.dtype != BF16]
    if todo:
        @pl.when(m == 0)
        def _():
            for w_ref, keep_ref in todo:
                keep_ref[...] = w_ref[...].astype(BF16)

    return [w_ref if w_ref.dtype == BF16 else keep_ref for w_ref, keep_ref in pairs]


def _weight_spec(w, layer, k, tn, index, **mode):
    if w.dtype == BF16:
        return pl.BlockSpec((k, tn), lambda j, m: (0, index(j)), **mode)
    return pl.BlockSpec((None, k, tn), lambda j, m: (layer, 0, index(j)), **mode)


def _mm_kernel(*refs, has_bias, has_res, tm, rs, per_row):
    it = iter(refs)
    x_ref, w_ref = next(it), next(it)
    b_ref = next(it) if has_bias else None
    res_ref, g_ref = (next(it), next(it)) if has_res else (None, None)
    o_ref, keep_ref = next(it), next(it, None)
    m = pl.program_id(1)
    (rhs_ref,) = _bf16_tiles(m, (w_ref, keep_ref))
    acc = jnp.dot(x_ref[...], rhs_ref[...], preferred_element_type=F32)
    if has_bias:
        acc = acc + b_ref[...]
    if has_res:
        acc = res_ref[...] + _mod_value(g_ref, m, tm, rs, per_row) * acc
    if len(o_ref.shape) == 3:
        for c in range(o_ref.shape[0]):
            o_ref[c] = acc[:, c * HEAD_DIM:(c + 1) * HEAD_DIM].astype(o_ref.dtype)
    else:
        o_ref[...] = acc.astype(o_ref.dtype)


def _mm(x, w, layer, *, out_dtype, bias=None, res=None, gate=None, tm_pref=1024, tn_pref=512,
        head_major=False, single_buffer_w=False, emit_bf16=False):
    mrows, k = x.shape
    n = w.shape[-1]
    tm = _row_tile(mrows, tm_pref)
    tn = _col_tile(n, tn_pref)
    w_mode = dict(pipeline_mode=pl.Buffered(1)) if single_buffer_w else {}
    in_specs = [pl.BlockSpec((tm, k), lambda j, m: (m, 0)),
                _weight_spec(w, layer, k, tn, lambda j: j, **w_mode)]
    args = [x, w]
    if bias is not None:
        in_specs.append(pl.BlockSpec((None, 1, tn), lambda j, m: (layer, 0, j)))
        args.append(bias.reshape(bias.shape[0], 1, n))
    rs, per_row = 1, True
    if res is not None:
        mod, gl, gate_chunk = gate
        rs, per_row = mod.rs, mod.per_row
        goff = gate_chunk * (n // tn)
        in_specs.append(pl.BlockSpec((tm, tn), lambda j, m: (m, j)))
        if per_row:
            in_specs.append(pl.BlockSpec((None, tm, tn), lambda j, m: (gl, m, goff + j)))
        else:
            in_specs.append(pl.BlockSpec((None, ADA_ROWS, tn), lambda j, m: (gl, 0, goff + j)))
        args += [res, mod.arr]
    if head_major:
        out_spec = pl.BlockSpec((tn // HEAD_DIM, tm, HEAD_DIM), lambda j, m: (j, m, 0))
        out_shape = jax.ShapeDtypeStruct((n // HEAD_DIM, mrows, HEAD_DIM), out_dtype)
    else:
        out_spec = pl.BlockSpec((tm, tn), lambda j, m: (m, j))
        out_shape = jax.ShapeDtypeStruct((mrows, n), out_dtype)
    scratch = []
    if emit_bf16:
        out_spec = (out_spec, pl.BlockSpec((k, tn), lambda j, m: (0, j)))
        out_shape = (out_shape, jax.ShapeDtypeStruct((k, n), BF16))
    elif w.dtype != BF16:
        scratch = [pltpu.VMEM((k, tn), BF16)]
    return pl.pallas_call(
        functools.partial(_mm_kernel, has_bias=bias is not None, has_res=res is not None,
                          tm=tm, rs=rs, per_row=per_row),
        grid=(n // tn, mrows // tm),
        in_specs=in_specs,
        out_specs=out_spec,
        out_shape=out_shape,
        scratch_shapes=scratch,
        compiler_params=_params("arbitrary", "arbitrary"),
        name="matmul",
    )(*args)


def _causal_taps(u, dw_ref, row, p0, p1):
    u1 = jnp.where(row == 0, p1, pltpu.roll(u, 1, 0))
    u2 = jnp.where(row == 0, p0, jnp.where(row == 1, p1, pltpu.roll(u, 2, 0)))
    return dw_ref[0:1, :] * u2 + dw_ref[1:2, :] * u1 + dw_ref[2:3, :] * u


def _pair_kernel(*refs, mode, rs):
    it = iter(refs)
    x_ref, wa_ref, wb_ref = next(it), next(it), next(it)
    if mode == "glu":
        ba_ref, bb_ref = next(it), next(it)
    else:
        dwa_ref, dwb_ref = next(it), next(it)
        c0a_ref, c0b_ref, c1a_ref, c1b_ref = next(it), next(it), next(it), next(it)
    o_ref = next(it)
    if mode == "ffn":
        ta_ref, tb_ref = next(it), next(it)
    keep_a_ref, keep_b_ref = next(it, None), next(it, None)
    m = pl.program_id(1)
    rhs_a_ref, rhs_b_ref = _bf16_tiles(m, (wa_ref, keep_a_ref), (wb_ref, keep_b_ref))
    x = x_ref[...]
    ua = jnp.dot(x, rhs_a_ref[...], preferred_element_type=F32)
    ub = jnp.dot(x, rhs_b_ref[...], preferred_element_type=F32)

    if mode == "glu":
        o_ref[...] = ((ua + ba_ref[...]) * jax.nn.sigmoid(ub + bb_ref[...])).astype(o_ref.dtype)
        return

    assert rs & (rs - 1) == 0
    row = lax.broadcasted_iota(jnp.int32, ua.shape, 0) & (rs - 1)
    gate = _causal_taps(ua, dwa_ref, row, c0a_ref[...], c1a_ref[...])
    val = _causal_taps(ub, dwb_ref, row, c0b_ref[...], c1b_ref[...])
    ta_ref[...] = ua
    tb_ref[...] = ub
    o_ref[...] = ((gate * jax.nn.sigmoid(gate)) * val).astype(o_ref.dtype)


def _pair_weights(w, layer, k, half, tn):
    if isinstance(w, tuple):
        return [_weight_spec(w[0], layer, k, tn, lambda j: j), _weight_spec(w[1], layer, k, tn, lambda j: j)], list(w)
    hoff = half // tn
    return [_weight_spec(w, layer, k, tn, lambda j: j), _weight_spec(w, layer, k, tn, lambda j: hoff + j)], [w, w]


def _pair_mm(x, w, layer, *, mode, rs, bias=None, dw=None, ctx=None, tm_pref=1024, tn_pref=256, emit_bf16=False):
    mrows, k = x.shape
    rounded = isinstance(w, tuple)
    half = w[0].shape[1] if rounded else w.shape[2] // 2
    n2 = 2 * half
    tm = _row_tile(mrows, tm_pref)
    tn = _col_tile(half, tn_pref)
    hoff = half // tn
    assert mode == "glu" or tm % rs == 0
    w_specs, w_args = _pair_weights(w, layer, k, half, tn)
    in_specs = [pl.BlockSpec((tm, k), lambda j, m: (m, 0))] + w_specs
    args = [x] + w_args
    scratch = [] if rounded or emit_bf16 else [pltpu.VMEM((k, tn), BF16), pltpu.VMEM((k, tn), BF16)]
    if mode == "glu":
        b3 = bias.reshape(bias.shape[0], 1, n2)
        in_specs += [pl.BlockSpec((None, 1, tn), lambda j, m: (layer, 0, j)),
                     pl.BlockSpec((None, 1, tn), lambda j, m: (layer, 0, hoff + j))]
        args += [b3, b3]
        out_shape = jax.ShapeDtypeStruct((mrows, half), F32)
        out_specs = pl.BlockSpec((tm, tn), lambda j, m: (m, j))
        if emit_bf16:
            out_shape = (out_shape,) + (jax.ShapeDtypeStruct((k, half), BF16),) * 2
            out_specs = (out_specs,) + (pl.BlockSpec((k, tn), lambda j, m: (0, j)),) * 2
    else:
        taps = dw.shape[1]
        in_specs += [pl.BlockSpec((None, taps, tn), lambda j, m: (layer, 0, j)),
                     pl.BlockSpec((None, taps, tn), lambda j, m: (layer, 0, hoff + j))]
        args += [dw, dw]
        c0, c1 = ctx
        lo = pl.BlockSpec((tm, tn), lambda j, m: (m, j))
        hi = pl.BlockSpec((tm, tn), lambda j, m: (m, hoff + j))
        in_specs += [lo, hi, lo, hi]
        args += [c0, c0, c1, c1]
        out_shape = (jax.ShapeDtypeStruct((mrows, half), BF16),) + (jax.ShapeDtypeStruct((mrows, half), F32),) * 2
        out_specs = (lo, lo, lo)
    return pl.pallas_call(
        functools.partial(_pair_kernel, mode=mode, rs=rs),
        grid=(half // tn, mrows // tm),
        in_specs=in_specs,
        out_specs=out_specs,
        out_shape=out_shape,
        scratch_shapes=scratch,
        compiler_params=_params("arbitrary", "arbitrary"),
        name="pair_" + mode,
    )(*args)


_FFN_CHUNK_ROWS = 256


def _ffn_up_kernel(x_ref, wa_ref, wb_ref, dwa_ref, dwb_ref, o_ref, ta_ref, tb_ref,
                   wabf_ref, wbbf_ref, ua_ref, ub_ref, *, tm, rs):
    m = pl.program_id(1)
    rc = min(_FFN_CHUNK_ROWS, tm)

    _bf16_tiles(m, (wa_ref, wabf_ref), (wb_ref, wbbf_ref))

    @pl.when((m * tm) % rs == 0)
    def _():
        ua_ref[0:8, :] = jnp.zeros((8, ua_ref.shape[1]), F32)
        ub_ref[0:8, :] = jnp.zeros((8, ub_ref.shape[1]), F32)

    x = x_ref[...]
    ua_ref[8:tm + 8, :] = jnp.dot(x, wabf_ref[...], preferred_element_type=F32)
    ub_ref[8:tm + 8, :] = jnp.dot(x, wbbf_ref[...], preferred_element_type=F32)

    def chunk(c, carry):
        r0 = pl.multiple_of(c * rc, rc)

        def taps(u_ref, dw_ref):
            u = u_ref[pl.ds(r0, rc + 8), :]
            y = dw_ref[0:1, :] * pltpu.roll(u, 2, 0) + dw_ref[1:2, :] * pltpu.roll(u, 1, 0) + dw_ref[2:3, :] * u
            return y[8:, :]

        gate = taps(ua_ref, dwa_ref)
        val = taps(ub_ref, dwb_ref)
        o_ref[pl.ds(r0, rc), :] = ((gate * jax.nn.sigmoid(gate)) * val).astype(o_ref.dtype)
        return carry

    lax.fori_loop(0, tm // rc, chunk, 0)

    @pl.when(((m + 1) * tm) % rs == 0)
    def _():
        ta_ref[...] = ua_ref[tm:tm + 8, :]
        tb_ref[...] = ub_ref[tm:tm + 8, :]

    ua_ref[0:8, :] = ua_ref[tm:tm + 8, :]
    ub_ref[0:8, :] = ub_ref[tm:tm + 8, :]


def _ffn_up(x, w, layer, dw, *, rs, tm_pref=1024, tn_pref=256):
    mrows, k = x.shape
    half = w.shape[2] // 2
    taps = dw.shape[1]
    tm = _row_tile(mrows, tm_pref)
    tn = _col_tile(half, tn_pref)
    hoff = half // tn
    assert rs % tm == 0 and taps == 3
    tail_shape = jax.ShapeDtypeStruct((mrows // rs, 8, half), F32)
    tail_spec = pl.BlockSpec((None, 8, tn), lambda j, m: ((m * tm) // rs, 0, j))
    wbf_shape = jax.ShapeDtypeStruct((k, half), BF16)
    wbf_spec = pl.BlockSpec((k, tn), lambda j, m: (0, j))
    return pl.pallas_call(
        functools.partial(_ffn_up_kernel, tm=tm, rs=rs),
        grid=(half // tn, mrows // tm),
        in_specs=[pl.BlockSpec((tm, k), lambda j, m: (m, 0)),
                  pl.BlockSpec((None, k, tn), lambda j, m: (layer, 0, j)),
                  pl.BlockSpec((None, k, tn), lambda j, m: (layer, 0, hoff + j)),
                  pl.BlockSpec((None, taps, tn), lambda j, m: (layer, 0, j)),
                  pl.BlockSpec((None, taps, tn), lambda j, m: (layer, 0, hoff + j))],
        out_specs=(pl.BlockSpec((tm, tn), lambda j, m: (m, j)), tail_spec, tail_spec, wbf_spec, wbf_spec),
        out_shape=(jax.ShapeDtypeStruct((mrows, half), BF16), tail_shape, tail_shape, wbf_shape, wbf_shape),
        scratch_shapes=[pltpu.VMEM((tm + 8, tn), F32)] * 2,
        compiler_params=_params("arbitrary", "arbitrary"),
        name="ffn_up",
    )(x, w, w, dw, dw)


def _band_bias(n_keys, n_steps):
    qi = lax.broadcasted_iota(jnp.int32, (Q_BLOCK, n_keys), 0)
    ki = lax.broadcasted_iota(jnp.int32, (Q_BLOCK, n_keys), 1)
    dist = qi + (n_keys - Q_BLOCK) - ki
    return jnp.where((dist >= 0) & (dist <= n_steps), 0.0, -jnp.inf).astype(F32)


def _attn_prompt_kernel(*refs, seq):
    qkv_refs, out_ref, scr = refs[:3 * N_GROUPS], refs[3 * N_GROUPS], refs[3 * N_GROUPS + 1:]
    scale = HEAD_DIM ** -0.5
    nt = (((1,), (1,)), ((), ()))
    for g, (window, dil) in enumerate(DILATED_GROUPS):
        q_ref, k_ref, v_ref = qkv_refs[3 * g:3 * g + 3]
        o_scr, l_scr = scr[2 * g], scr[2 * g + 1]
        n_steps = window // dil
        bias_first = _band_bias(Q_BLOCK, n_steps)
        bias_full = _band_bias(2 * Q_BLOCK, n_steps)
        for r in range(dil):
            for blk in range(seq // dil // Q_BLOCK):
                start = r + dil * Q_BLOCK * blk

                def rows(first, count):
                    return pl.ds(first, count, stride=dil) if dil > 1 else pl.ds(first, count)

                q = q_ref[rows(start, Q_BLOCK), :].astype(BF16)
                if blk == 0:
                    keys, bias = rows(start, Q_BLOCK), bias_first
                else:
                    keys, bias = rows(start - dil * Q_BLOCK, 2 * Q_BLOCK), bias_full
                k = k_ref[keys, :].astype(BF16)
                v = v_ref[keys, :].astype(BF16)
                s = lax.dot_general(q, k, nt, preferred_element_type=F32) * scale + bias
                mx = jnp.max(s, axis=-1, keepdims=True)
                p = jnp.exp(s - mx)
                den = jnp.sum(p, axis=-1, keepdims=True)
                o_scr[rows(start, Q_BLOCK), :] = jnp.dot(p.astype(BF16), v, preferred_element_type=F32) / den
                l_scr[rows(start, Q_BLOCK), :] = jnp.broadcast_to(mx + jnp.log(den), (Q_BLOCK, HEAD_DIM))
    l1, l2, l3 = scr[1][...], scr[3][...], scr[5][...]
    mx = jnp.maximum(jnp.maximum(l1, l2), l3)
    e1, e2, e3 = jnp.exp(l1 - mx), jnp.exp(l2 - mx), jnp.exp(l3 - mx)
    num = e1 * scr[0][...] + e2 * scr[2][...] + e3 * scr[4][...]
    out_ref[...] = (num / (e1 + e2 + e3)).astype(out_ref.dtype)


def _attn_prompt(qkv_t, nseq, seq):
    heads = qkv_t.shape[0] // (3 * N_GROUPS)
    for window, dil in DILATED_GROUPS:
        assert seq % (dil * Q_BLOCK) == 0 and window // dil <= Q_BLOCK
    in_specs = [pl.BlockSpec((None, seq, HEAD_DIM), lambda b, h, c=c: (c * heads + h, b, 0))
                for c in range(3 * N_GROUPS)]
    return pl.pallas_call(
        functools.partial(_attn_prompt_kernel, seq=seq),
        grid=(nseq, heads),
        in_specs=in_specs,
        out_specs=pl.BlockSpec((seq, HEAD_DIM), lambda b, h: (b, h)),
        out_shape=jax.ShapeDtypeStruct((nseq * seq, heads * HEAD_DIM), BF16),
        scratch_shapes=[pltpu.VMEM((seq, HEAD_DIM), F32)] * (2 * N_GROUPS),
        compiler_params=_params("arbitrary", "arbitrary"),
        name="attn_prompt",
    )(*([qkv_t] * (3 * N_GROUPS)))


def _attn_sample_kernel(*refs, heads, steps):
    qkv_refs, cache_refs, o_ref = refs[:3 * N_GROUPS], refs[3 * N_GROUPS:4 * N_GROUPS], refs[4 * N_GROUPS]
    assert heads & (heads - 1) == 0
    t = pl.program_id(1)
    scale = HEAD_DIM ** -0.5
    nt = (((1,), (1,)), ((), ()))
    log_heads = jnp.int32(heads.bit_length() - 1)

    def columns(n_pos):
        col = lax.broadcasted_iota(jnp.int32, (heads, n_pos * heads), 1)
        row = lax.broadcasted_iota(jnp.int32, (heads, n_pos * heads), 0)
        return (col & (heads - 1)) == row, lax.shift_right_logical(col, log_heads)

    own_c, pos_c = columns(Q_BLOCK)
    own_n, pos_n = columns(steps)
    scores, values = [], []
    for g, (_, dil) in enumerate(DILATED_GROUPS):
        q_ref, kn_ref, vn_ref = qkv_refs[3 * g:3 * g + 3]
        c_ref = cache_refs[g]
        q = q_ref[...].astype(BF16)
        kc = c_ref[:, 0:heads, :].reshape(Q_BLOCK * heads, HEAD_DIM).astype(BF16)
        kn = kn_ref[...].reshape(steps * heads, HEAD_DIM).astype(BF16)
        sc = lax.dot_general(q, kc, nt, preferred_element_type=F32) * scale
        sn = lax.dot_general(q, kn, nt, preferred_element_type=F32) * scale
        sc = jnp.where(own_c & (pos_c >= t // dil), sc, -jnp.inf)
        back = t - pos_n
        sn = jnp.where(own_n & (back >= 0) & ((back & (dil - 1)) == 0), sn, -jnp.inf)
        scores += [sc, sn]
        values += [c_ref[:, heads:2 * heads, :].reshape(Q_BLOCK * heads, HEAD_DIM).astype(BF16),
                   vn_ref[...].reshape(steps * heads, HEAD_DIM).astype(BF16)]
    mx = functools.reduce(jnp.maximum, [jnp.max(s, axis=-1, keepdims=True) for s in scores])
    den = jnp.zeros((heads, 1), F32)
    acc = jnp.zeros((heads, HEAD_DIM), F32)
    for s, v in zip(scores, values):
        p = jnp.exp(s - mx)
        den = den + jnp.sum(p, axis=-1, keepdims=True)
        acc = acc + jnp.dot(p.astype(BF16), v, preferred_element_type=F32)
    o_ref[...] = acc / den


def _attn_sample(qkv, caches, layer, nseq, steps):
    aw = qkv.shape[1] // (3 * N_GROUPS)
    heads = aw // HEAD_DIM
    qkv4 = qkv.reshape(nseq, steps, 3 * N_GROUPS * heads, HEAD_DIM)
    in_specs, args = [], []
    for g in range(N_GROUPS):
        in_specs += [pl.BlockSpec((None, None, heads, HEAD_DIM), lambda b, t, g=g: (b, t, 3 * g, 0)),
                     pl.BlockSpec((None, steps, heads, HEAD_DIM), lambda b, t, g=g: (b, 0, 3 * g + 1, 0)),
                     pl.BlockSpec((None, steps, heads, HEAD_DIM), lambda b, t, g=g: (b, 0, 3 * g + 2, 0))]
        args += [qkv4] * 3
    for (window, dil), cache in zip(DILATED_GROUPS, caches):
        assert cache.shape[2] == window and window // dil == Q_BLOCK
        args.append(cache.reshape(cache.shape[0] * nseq, Q_BLOCK, dil, 2 * heads, HEAD_DIM))
        in_specs.append(pl.BlockSpec((None, Q_BLOCK, None, 2 * heads, HEAD_DIM),
                                     lambda b, t, dil=dil: (layer * nseq + b, 0, t % dil, 0, 0)))
    out = pl.pallas_call(
        functools.partial(_attn_sample_kernel, heads=heads, steps=steps),
        grid=(nseq, steps),
        in_specs=in_specs,
        out_specs=pl.BlockSpec((None, None, heads, HEAD_DIM), lambda b, t: (b, t, 0, 0)),
        out_shape=jax.ShapeDtypeStruct((nseq, steps, heads, HEAD_DIM), F32),
        compiler_params=_params("arbitrary", "arbitrary"),
        name="attn_sample",
    )(*args)
    return out.reshape(nseq * steps, aw)


_SLIDE_BLOCK_BYTES = 8 * 1024 * 1024


def _slide_kernel(main_ref, next_ref, new_ref, o_ref, *, groups):
    i = pl.program_id(1)
    last = pl.num_programs(1) - 1
    if groups > 1:
        o_ref[0:groups - 1] = main_ref[1:groups]

    @pl.when(i < last)
    def _():
        o_ref[groups - 1] = next_ref[0]

    @pl.when(i == last)
    def _():
        o_ref[groups - 1] = new_ref[...]


def _slide_cache(cache, layer, new_rows):
    layers, nseq, window = cache.shape[:3]
    steps = new_rows.shape[1]
    rows = steps * cache.shape[3] * cache.shape[4]
    assert window % steps == 0 and rows % 8 == 0
    ngroups = window // steps
    groups = _row_tile(ngroups, max(1, _SLIDE_BLOCK_BYTES // (rows * HEAD_DIM * 4)))
    view = cache.reshape(layers * nseq, ngroups, rows, HEAD_DIM)
    blk = (None, groups, rows, HEAD_DIM)
    out = pl.pallas_call(
        functools.partial(_slide_kernel, groups=groups),
        grid=(nseq, ngroups // groups),
        in_specs=[pl.BlockSpec(blk, lambda b, i: (layer * nseq + b, i, 0, 0)),
                  pl.BlockSpec((None, 1, rows, HEAD_DIM),
                               lambda b, i: (layer * nseq + b, jnp.minimum((i + 1) * groups, ngroups - 1), 0, 0)),
                  pl.BlockSpec((None, rows, HEAD_DIM), lambda b, i: (b, 0, 0))],
        out_specs=pl.BlockSpec(blk, lambda b, i: (b, i, 0, 0)),
        out_shape=jax.ShapeDtypeStruct((nseq, ngroups, rows, HEAD_DIM), F32),
        compiler_params=_params("arbitrary", "arbitrary"),
        name="cache_slide",
    )(view, view, new_rows.reshape(nseq, rows, HEAD_DIM))
    return out.reshape((nseq,) + cache.shape[2:])


_CONV_ROWS = 128
_CONV_COLS = 128


def _dwln_kernel(u_ref, ctx_ref, w_ref, bdw_ref, g_ref, b_ref, o_ref, ext_ref, y_ref, *, tm, rs, taps, zero_start):
    m = pl.program_id(0)
    if zero_start:
        at_start = (m * tm) % rs == 0

        @pl.when(at_start)
        def _():
            ext_ref[0:CONV_CTX_ROWS, :] = jnp.zeros((CONV_CTX_ROWS, ext_ref.shape[1]), F32)

        @pl.when(jnp.logical_not(at_start))
        def _():
            ext_ref[0:CONV_CTX_ROWS, :] = ctx_ref[...]
    else:
        ext_ref[0:CONV_CTX_ROWS, :] = ctx_ref[...]
    ext_ref[CONV_CTX_ROWS:CONV_CTX_ROWS + tm, :] = u_ref[...]
    ext_ref[CONV_CTX_ROWS + tm:CONV_CTX_ROWS + tm + 8, :] = jnp.zeros((8, ext_ref.shape[1]), F32)

    d = u_ref.shape[1]
    rc = min(_CONV_ROWS, tm)
    cw = min(_CONV_COLS, d)
    first = CONV_CTX_ROWS - (taps - 1)
    for r0 in range(0, tm, rc):
        for c0 in range(0, d, cw):
            acc = jnp.zeros((rc, cw), F32)
            for s in range(8):
                part = None
                for k in range(taps):
                    if (first + k) % 8 != s:
                        continue
                    a = r0 + first + k - s
                    term = w_ref[k:k + 1, c0:c0 + cw] * ext_ref[a:a + rc + 8, c0:c0 + cw]
                    part = term if part is None else part + term
                if part is not None:
                    acc = acc + part[s:s + rc, :]
            y_ref[r0:r0 + rc, c0:c0 + cw] = acc + bdw_ref[:, c0:c0 + cw]

    y = y_ref[...]
    mu = jnp.mean(y, axis=-1, keepdims=True)
    yc = y - mu
    var = jnp.mean(yc * yc, axis=-1, keepdims=True)
    z = yc * lax.rsqrt(var + LN_EPS) * g_ref[...] + b_ref[...]
    o_ref[...] = (z * jax.nn.sigmoid(z)).astype(o_ref.dtype)


def _dwln(u, ctx, layer, w_dw, b_dw, ln_g, ln_b, *, rs, tm):
    mrows, d = u.shape
    taps = w_dw.shape[1]
    assert taps - 1 <= CONV_CTX_ROWS and tm % 8 == 0
    wpad = jnp.pad(w_dw, ((0, 0), (0, CONV_CTX_ROWS - taps), (0, 0)))
    zero_start = ctx is None
    if zero_start:
        assert tm % CONV_CTX_ROWS == 0
        per = tm // CONV_CTX_ROWS
        ctx_arr = u
        ctx_spec = pl.BlockSpec((CONV_CTX_ROWS, d), lambda m: (jnp.maximum(m * per - 1, 0), 0))
    else:
        assert tm == rs
        ctx_arr = ctx
        ctx_spec = pl.BlockSpec((None, CONV_CTX_ROWS, d), lambda m: (m, 0, 0))
    vec = lambda a: a.reshape(a.shape[0], 1, d)
    vec_spec = pl.BlockSpec((None, 1, d), lambda m: (layer, 0, 0))
    return pl.pallas_call(
        functools.partial(_dwln_kernel, tm=tm, rs=rs, taps=taps, zero_start=zero_start),
        grid=(mrows // tm,),
        in_specs=[pl.BlockSpec((tm, d), lambda m: (m, 0)), ctx_spec,
                  pl.BlockSpec((None, CONV_CTX_ROWS, d), lambda m: (layer, 0, 0)),
                  vec_spec, vec_spec, vec_spec],
        out_specs=pl.BlockSpec((tm, d), lambda m: (m, 0)),
        out_shape=jax.ShapeDtypeStruct((mrows, d), BF16),
        scratch_shapes=[pltpu.VMEM((CONV_CTX_ROWS + tm + 8, d), F32), pltpu.VMEM((tm, d), F32)],
        compiler_params=_params("arbitrary"),
        name="dwconv_ln",
    )(u, ctx_arr, wpad, vec(b_dw), vec(ln_g), vec(ln_b))


def _trunk(x, mod, nseq, rs, params, kv_caches, conv_states, ffn_states, rounded):
    (w_qkv, w_o, w_pw1, b_pw1, w_dw, b_dw, ln_g, ln_b, w_pw2, b_pw2,
     w_up, w_ffn_dw, w_down, final_norm_g) = params
    depth = w_up.shape[0]
    d = x.shape[1]
    prompt = kv_caches is None
    new_kv = [[] for _ in range(N_GROUPS)]
    new_conv, new_ffn = [], []

    def gated(name, xin, w, layer, **kw):
        if prompt:
            out, rounded[name, layer] = _mm(xin, w, layer, out_dtype=F32, emit_bf16=True, **kw)
            return out
        return _mm(xin, rounded[name, layer], layer, out_dtype=F32, **kw)

    for i in range(depth):
        j = i // 2
        h = _modulate(x, mod, i, 0)
        if i % 2 == 0:
            heads = w_qkv.shape[2] // (3 * N_GROUPS * HEAD_DIM)
            if prompt:
                qkv_t, rounded["qkv", j] = _mm(h, w_qkv, j, out_dtype=F32, head_major=True, emit_bf16=True)
                mixed = _attn_prompt(qkv_t, nseq, rs)
                qkv5 = qkv_t.reshape(N_GROUPS, 3, heads, nseq, rs, HEAD_DIM)
                for g, (window, _) in enumerate(DILATED_GROUPS):
                    keep = min(window, rs)
                    new_kv[g].append(jnp.transpose(qkv5[g, 1:, :, :, rs - keep:], (2, 3, 0, 1, 4)))
            else:
                qkv = _mm(h, rounded["qkv", j], j, out_dtype=F32)
                mixed = _attn_sample(qkv, kv_caches, j, nseq, rs).astype(BF16)
                qkv5 = qkv.reshape(nseq, rs, N_GROUPS, 3, heads, HEAD_DIM)
                for g in range(N_GROUPS):
                    new_kv[g].append(_slide_cache(kv_caches[g], j, qkv5[:, :, g, 1:]))
            x = gated("o", mixed, w_o, j, res=x, gate=(mod, i, 2))
        else:
            if prompt:
                u, *rounded["pw1", j] = _pair_mm(h, w_pw1, j, mode="glu", rs=rs, bias=b_pw1, emit_bf16=True)
            else:
                u = _pair_mm(h, tuple(rounded["pw1", j]), j, mode="glu", rs=rs, bias=b_pw1)
            taps = w_dw.shape[1]
            if prompt:
                y = _dwln(u, None, j, w_dw, b_dw, ln_g, ln_b, rs=rs, tm=128)
                new_conv.append(u.reshape(nseq, rs, d)[:, rs - (taps - 1):])
            else:
                state = conv_states[j]
                ctx = jnp.pad(state, ((0, 0), (CONV_CTX_ROWS - (taps - 1), 0), (0, 0)))
                y = _dwln(u, ctx, j, w_dw, b_dw, ln_g, ln_b, rs=rs, tm=rs)
                u_all = jnp.concatenate([state, u.reshape(nseq, rs, d)], axis=1)
                new_conv.append(u_all[:, -(taps - 1):])
            x = gated("pw2", y, w_pw2, j, bias=b_pw2, res=x, gate=(mod, i, 2))
        h = _modulate(x, mod, i, 3)
        if prompt:
            act, ta, tb, *rounded["up", i] = _ffn_up(h, w_up, i, w_ffn_dw, rs=rs)
            new_ffn.append(jnp.concatenate([ta[:, 6:8], tb[:, 6:8]], axis=-1))
        else:
            state = ffn_states[i]
            c0 = jnp.repeat(state[:, 0], rs, axis=0)
            c1 = jnp.repeat(state[:, 1], rs, axis=0)
            act, ta, tb = _pair_mm(h, tuple(rounded["up", i]), i, mode="ffn", rs=rs, dw=w_ffn_dw, ctx=(c0, c1))
            u_ffn = jnp.concatenate([ta, tb], axis=-1).reshape(nseq, rs, -1)
            new_ffn.append(jnp.concatenate([state, u_ffn], axis=1)[:, -state.shape[1]:])
        x = _mm(act, w_down, i, out_dtype=F32, res=x, gate=(mod, i, 5), tm_pref=256, tn_pref=512,
                single_buffer_w=True)
    y = _final_norm(x, final_norm_g)
    return y, [jnp.stack(l) for l in new_kv], jnp.stack(new_conv), jnp.stack(new_ffn)


def kernel(x_prompt, x_sample, c_prompt, c_sample, cache_kv_g1, cache_kv_g2, cache_kv_g3, state_conv, state_ffn_conv, w_ada, b_ada, w_qkv, w_o, w_pw1, b_pw1, w_dw, b_dw, ln_g, ln_b, w_pw2, b_pw2, w_up, w_ffn_dw, w_down, final_norm_g):
    nb, seq, d = x_prompt.shape
    ndb, steps, _ = x_sample.shape
    assert nb + ndb <= ADA_ROWS
    params = (w_qkv, w_o, w_pw1, b_pw1, w_dw, b_dw, ln_g, ln_b, w_pw2, b_pw2,
              w_up, w_ffn_dw, w_down, final_norm_g)

    c_rows = jnp.concatenate([c_prompt, c_sample, jnp.zeros((ADA_ROWS - nb - ndb, d), F32)], axis=0)
    ada = _adaln(c_rows, w_ada, b_ada)
    mod_p = _Mod(ada, False, seq)
    mod_s = _Mod(jnp.repeat(ada[:, nb:nb + ndb], steps, axis=1), True, steps)

    rounded = {}
    y_p, kv_p, conv_p, ffn_p = _trunk(x_prompt.reshape(nb * seq, d), mod_p, nb, seq, params, None, None, None,
                                      rounded)
    y_s, kv_s, conv_s, ffn_s = _trunk(x_sample.reshape(ndb * steps, d), mod_s, ndb, steps, params,
                                      (cache_kv_g1, cache_kv_g2, cache_kv_g3), state_conv, state_ffn_conv, rounded)
    return (y_p.reshape(nb, seq, d), y_s.reshape(ndb, steps, d), kv_p[0], kv_p[1], kv_p[2], conv_p, ffn_p,
            kv_s[0], kv_s[1], kv_s[2], conv_s, ffn_s)
```
